```python
import math
import jax, jax.numpy as jnp
from jax import lax
import numpy as np

D_MODEL = 1024
BATCH = 4
SEQ = 8192
DEPTH = 1
DEC_BATCH = 128
DEC_SEQ = 4
PAST_LEN = 16384
PAGE_SIZE = 128

A_HEADS = 8
A_NOPE = 64
A_ROPE = 32
A_V = 64
A_Q_RANK = 768
A_KV_RANK = 256
A_SCALE = 1.0 / math.sqrt(A_NOPE + A_ROPE)
B_HEADS = 8
B_KV_HEADS = 4
B_GROUP = B_HEADS // B_KV_HEADS
B_HEAD_DIM = 64
B_SCALE = 1.0 / math.sqrt(B_HEAD_DIM)
IDX_HEADS = 8
IDX_DIM = 64
IDX_SCALE = 1.0 / math.sqrt(IDX_DIM)
TOPK_MAX = 256
D_FF = 4 * D_MODEL
ROPE_THETA = 10000.0
EPS = 1e-6
Q_BLOCK = 128
NEG_INIT = -1e30
IN_SIZES = (A_Q_RANK, A_KV_RANK, A_ROPE, B_HEADS * B_HEAD_DIM, B_KV_HEADS * B_HEAD_DIM,
            B_KV_HEADS * B_HEAD_DIM, IDX_HEADS * IDX_DIM, IDX_DIM, IDX_HEADS, 2 * D_MODEL)
D_IN = sum(IN_SIZES)

kernel_name = "hybrid_mla_dsa_gated_decoder_step"


def rms_norm(x, g):
    xf = x.astype(jnp.float32)
    y = xf * lax.rsqrt(jnp.mean(xf * xf, axis=-1, keepdims=True) + EPS)
    return (y * g.astype(jnp.float32)).astype(x.dtype)


def rope(x, pos):
    d = x.shape[-1]
    inv = ROPE_THETA ** (-jnp.arange(0, d, 2, dtype=jnp.float32) / d)
    ang = pos.astype(jnp.float32)[:, None] * inv[None, :]
    if x.ndim == 4:
        ang = ang[:, None, :]
    cos, sin = jnp.cos(ang), jnp.sin(ang)
    xf = x.astype(jnp.float32)
    x1, x2 = xf[..., : d // 2], xf[..., d // 2:]
    return jnp.concatenate([x1 * cos - x2 * sin, x2 * cos + x1 * sin], axis=-1).astype(x.dtype)


def mixer_inputs(h, pos, lp):
    N, T = h.shape[0], h.shape[1]
    n = rms_norm(h, lp['g_attn'])
    proj = n @ lp['w_in']
    offs = np.cumsum(IN_SIZES)[:-1].tolist()
    q_a, c_kv, kr, qb, kb, vb, qi, ki, wi, gate = jnp.split(proj, offs, axis=-1)
    qa = (rms_norm(q_a, lp['g_q_a']) @ lp['w_uq']).reshape(N, T, A_HEADS, A_NOPE + A_ROPE)
    q_nope = rms_norm(qa[..., :A_NOPE], lp['g_qn_a'])
    q_rope = rope(rms_norm(qa[..., A_NOPE:], lp['g_qr_a']), pos)
    c = rms_norm(c_kv, lp['g_kv_a'])
    kr = rope(rms_norm(kr, lp['g_kr_a']), pos)
    qb = rope(rms_norm(qb.reshape(N, T, B_HEADS, B_HEAD_DIM), lp['g_q_b']), pos)
    kb = rope(rms_norm(kb.reshape(N, T, B_KV_HEADS, B_HEAD_DIM), lp['g_k_b']), pos)
    vb = vb.reshape(N, T, B_KV_HEADS, B_HEAD_DIM)
    qi = rope(qi.reshape(N, T, IDX_HEADS, IDX_DIM), pos)
    ki = rope(ki, pos)
    wi = wi * (IDX_HEADS ** -0.5)
    return dict(q_nope=q_nope, q_rope=q_rope, c=c, kr=kr, qb=qb, kb=kb, vb=vb,
                qi=qi, ki=ki, wi=wi, gate=jax.nn.sigmoid(gate))


def mla_kv(c, lp):
    k_nope = rms_norm(jnp.einsum('nsc,chd->nshd', c, lp['w_uk']), lp['g_kn_a'])
    v = jnp.einsum('nsc,chd->nshd', c, lp['w_uv'])
    return k_nope, v


def mla_scores(q_nope, q_rope, k_nope, kr):
    s = jnp.einsum('nthd,nshd->nths', q_nope, k_nope) + jnp.einsum('nthd,nsd->nths', q_rope, kr)
    return s.astype(jnp.float32) * A_SCALE


def mla_prompt(m, lp):
    q_nope, q_rope = m['q_nope'], m['q_rope']
    N, S = q_nope.shape[0], q_nope.shape[1]
    k_nope, v = mla_kv(m['c'], lp)
    kr = m['kr']
    nb = S // Q_BLOCK
    qn_b = q_nope.reshape(N, nb, Q_BLOCK, A_HEADS, A_NOPE).swapaxes(0, 1)
    qr_b = q_rope.reshape(N, nb, Q_BLOCK, A_HEADS, A_ROPE).swapaxes(0, 1)
    kpos = jnp.arange(S)

    def blk(args):
        qn, qr, start = args
        s = mla_scores(qn, qr, k_nope, kr)
        qpos = start + jnp.arange(Q_BLOCK)
        mask = kpos[None, :] <= qpos[:, None]
        s = jnp.where(mask[None, :, None, :], s, -jnp.inf)
        p = jax.nn.softmax(s, axis=-1).astype(v.dtype)
        return jnp.einsum('nths,nshd->nthd', p, v)

    out = lax.map(blk, (qn_b, qr_b, jnp.arange(nb) * Q_BLOCK))
    return out.swapaxes(0, 1).reshape(N, S, A_HEADS * A_V)


def online_update(carry, s, v):
    mx, l, acc = carry
    m_new = jnp.maximum(mx, s.max(axis=-1))
    corr = jnp.exp(mx - m_new)
    p = jnp.exp(s - m_new[..., None])
    acc = acc * corr[..., None] + jnp.einsum('nths,nshd->nthd', p, v.astype(jnp.float32))
    return (m_new, l * corr + p.sum(axis=-1), acc)


def mla_sample(m, cache_lat, cache_kr, page_table, lp):
    q_nope, q_rope = m['q_nope'], m['q_rope']
    N, T = q_nope.shape[0], q_nope.shape[1]
    init = (jnp.full((N, T, A_HEADS), NEG_INIT, jnp.float32),
            jnp.zeros((N, T, A_HEADS), jnp.float32),
            jnp.zeros((N, T, A_HEADS, A_V), jnp.float32))

    def step(carry, phys):
        c = cache_lat[phys]
        kr = cache_kr[phys]
        k_nope, v = mla_kv(c, lp)
        s = mla_scores(q_nope, q_rope, k_nope, kr)
        return online_update(carry, s, v), None

    carry, _ = lax.scan(step, init, page_table.T)
    k_nope, v = mla_kv(m['c'], lp)
    s = mla_scores(q_nope, q_rope, k_nope, m['kr'])
    causal = jnp.arange(T)[None, :] <= jnp.arange(T)[:, None]
    s = jnp.where(causal[None, :, None, :], s, -jnp.inf)
    mx, l, acc = online_update(carry, s, v)
    out = (acc / l[..., None]).astype(q_nope.dtype)
    return out.reshape(N, T, A_HEADS * A_V)


def indexer_scores(qi, ik, wi):
    sc = jax.nn.relu(jnp.einsum('nthd,nsd->nths', qi, ik).astype(jnp.float32) * IDX_SCALE)
    return jnp.einsum('nths,nth->nts', sc, wi.astype(jnp.float32))


def sparse_attend(q, ksel, vsel, valid):
    N, T = q.shape[0], q.shape[1]
    qg = q.reshape(N, T, B_KV_HEADS, B_GROUP, B_HEAD_DIM)
    s = jnp.einsum('ntgrd,ntkgd->ntgrk', qg, ksel).astype(jnp.float32) * B_SCALE
    s = jnp.where(valid[:, :, None, None, :], s, -jnp.inf)
    p = jax.nn.softmax(s, axis=-1).astype(vsel.dtype)
    o = jnp.einsum('ntgrk,ntkgd->ntgrd', p, vsel)
    return o.reshape(N, T, B_HEADS * B_HEAD_DIM)


def dsa_prompt(m):
    qb, kb, vb, qi, ki, wi = m['qb'], m['kb'], m['vb'], m['qi'], m['ki'], m['wi']
    N, S = qb.shape[0], qb.shape[1]
    n_sel = min(TOPK_MAX, S // 4)
    nb = S // Q_BLOCK
    q_b = qb.reshape(N, nb, Q_BLOCK, B_HEADS, B_HEAD_DIM).swapaxes(0, 1)
    qi_b = qi.reshape(N, nb, Q_BLOCK, IDX_HEADS, IDX_DIM).swapaxes(0, 1)
    wi_b = wi.reshape(N, nb, Q_BLOCK, IDX_HEADS).swapaxes(0, 1)
    kpos = jnp.arange(S)
    gather = jax.vmap(lambda a, i: a[i])

    def blk(args):
        q, qi_, w, start = args
        qpos = start + jnp.arange(Q_BLOCK)
        I = indexer_scores(qi_, ki, w)
        visible = kpos[None, :] <= qpos[:, None]
        I = jnp.where(visible[None], I, -jnp.inf)
        _, idx = lax.top_k(I, n_sel)
        valid = idx <= qpos[None, :, None]
        return sparse_attend(q, gather(kb, idx), gather(vb, idx), valid)

    out = lax.map(blk, (q_b, qi_b, wi_b, jnp.arange(nb) * Q_BLOCK))
    return out.swapaxes(0, 1).reshape(N, S, B_HEADS * B_HEAD_DIM)


def dsa_sample(m, cache_k, cache_v, cache_ik, page_table):
    qb, kb, vb, qi, ki, wi = m['qb'], m['kb'], m['vb'], m['qi'], m['ki'], m['wi']
    N, T = qb.shape[0], qb.shape[1]
    psz = cache_ik.shape[1]
    past = page_table.shape[1] * psz
    L = past + T
    n_sel = min(TOPK_MAX, L // 4)
    ik_all = jnp.concatenate([cache_ik[page_table].reshape(N, past, IDX_DIM), ki], axis=1)
    I = indexer_scores(qi, ik_all, wi)
    qpos = past + jnp.arange(T)
    visible = jnp.arange(L)[None, :] <= qpos[:, None]
    I = jnp.where(visible[None], I, -jnp.inf)
    _, idx = lax.top_k(I, n_sel)
    valid = idx <= qpos[None, :, None]
    is_new = (idx >= past)[..., None, None]
    pidx = jnp.minimum(idx, past - 1)
    phys = page_table[jnp.arange(N)[:, None, None], pidx // psz]
    off = pidx % psz
    nidx = jnp.clip(idx - past, 0, T - 1)
    gather = jax.vmap(lambda a, i: a[i])
    ksel = jnp.where(is_new, gather(kb, nidx), cache_k[phys, off])
    vsel = jnp.where(is_new, gather(vb, nidx), cache_v[phys, off])
    return sparse_attend(qb, ksel, vsel, valid)


def layer_out(h, m, o_a, o_b, lp):
    y_a = o_a @ lp['w_o_a']
    y_b = o_b @ lp['w_o_b']
    g_a, g_b = jnp.split(m['gate'], 2, axis=-1)
    h = h + (g_a * y_a + g_b * y_b) @ lp['w_out']
    u = jax.nn.relu(rms_norm(h, lp['g_mlp']) @ lp['w_up'])
    return h + (u * u) @ lp['w_down']


def setup_inputs(seed: int = 0) -> dict:
    key = jax.random.key(seed)
    ks = jax.random.split(key, 32)
    f32 = jnp.float32
    n_pages = PAST_LEN // PAGE_SIZE
    n_used = DEC_BATCH * n_pages
    n_pool = n_used + max(1, n_used // 4)

    def nrm(k, shape, scale=1.0):
        return jax.random.normal(k, shape, f32) * scale

    def gain(k, d):
        return 1.0 + 0.02 * jax.random.normal(k, (DEPTH, d), f32)

    page_table = jax.random.permutation(ks[7], n_pool)[:n_used].reshape(DEC_BATCH, n_pages).astype(jnp.int32)
    return {
        'x_prompt': nrm(ks[0], (BATCH, SEQ, D_MODEL)),
        'x_sample': nrm(ks[1], (DEC_BATCH, DEC_SEQ, D_MODEL)),
        'cache_mla_latent': nrm(ks[2], (DEPTH, n_pool, PAGE_SIZE, A_KV_RANK)),
        'cache_mla_krope': nrm(ks[3], (DEPTH, n_pool, PAGE_SIZE, A_ROPE)),
        'cache_dsa_k': nrm(ks[4], (DEPTH, n_pool, PAGE_SIZE, B_KV_HEADS, B_HEAD_DIM)),
        'cache_dsa_v': nrm(ks[5], (DEPTH, n_pool, PAGE_SIZE, B_KV_HEADS, B_HEAD_DIM)),
        'cache_idx_k': nrm(ks[6], (DEPTH, n_pool, PAGE_SIZE, IDX_DIM)),
        'page_table': page_table,
        'g_attn': gain(ks[8], D_MODEL),
        'w_in': nrm(ks[9], (DEPTH, D_MODEL, D_IN), D_MODEL ** -0.5),
        'g_q_a': gain(ks[10], A_Q_RANK),
        'w_uq': nrm(ks[11], (DEPTH, A_Q_RANK, A_HEADS * (A_NOPE + A_ROPE)), A_Q_RANK ** -0.5),
        'g_qn_a': gain(ks[12], A_NOPE),
        'g_qr_a': gain(ks[13], A_ROPE),
        'g_kv_a': gain(ks[14], A_KV_RANK),
        'g_kr_a': gain(ks[15], A_ROPE),
        'w_uk': nrm(ks[16], (DEPTH, A_KV_RANK, A_HEADS, A_NOPE), A_KV_RANK ** -0.5),
        'g_kn_a': gain(ks[17], A_NOPE),
        'w_uv': nrm(ks[18], (DEPTH, A_KV_RANK, A_HEADS, A_V), A_KV_RANK ** -0.5),
        'w_o_a': nrm(ks[19], (DEPTH, A_HEADS * A_V, D_MODEL), (A_HEADS * A_V) ** -0.5),
        'g_q_b': gain(ks[20], B_HEAD_DIM),
        'g_k_b': gain(ks[21], B_HEAD_DIM),
        'w_o_b': nrm(ks[22], (DEPTH, B_HEADS * B_HEAD_DIM, D_MODEL), (B_HEADS * B_HEAD_DIM) ** -0.5),
        'w_out': nrm(ks[23], (DEPTH, D_MODEL, D_MODEL), D_MODEL ** -0.5),
        'g_mlp': gain(ks[24], D_MODEL),
        'w_up': nrm(ks[25], (DEPTH, D_MODEL, D_FF), D_MODEL ** -0.5),
        'w_down': nrm(ks[26], (DEPTH, D_FF, D_MODEL), D_FF ** -0.5),
    }


def reference(x_prompt, x_sample, cache_mla_latent, cache_mla_krope, cache_dsa_k, cache_dsa_v,
              cache_idx_k, page_table, g_attn, w_in, g_q_a, w_uq, g_qn_a, g_qr_a, g_kv_a, g_kr_a,
              w_uk, g_kn_a, w_uv, w_o_a, g_q_b, g_k_b, w_o_b, w_out, g_mlp, w_up, w_down):
    S = x_prompt.shape[1]
    T = x_sample.shape[1]
    past = page_table.shape[1] * cache_mla_latent.shape[2]
    pos_p = jnp.arange(S)
    pos_s = past + jnp.arange(T)
    h_p, h_s = x_prompt, x_sample
    lat_p, kr_p, k_p, v_p, ik_p = [], [], [], [], []
    lat_s, kr_s, k_s, v_s, ik_s = [], [], [], [], []
    for l in range(DEPTH):
        lp = dict(g_attn=g_attn[l], w_in=w_in[l], g_q_a=g_q_a[l], w_uq=w_uq[l], g_qn_a=g_qn_a[l],
                  g_qr_a=g_qr_a[l], g_kv_a=g_kv_a[l], g_kr_a=g_kr_a[l], w_uk=w_uk[l], g_kn_a=g_kn_a[l],
                  w_uv=w_uv[l], w_o_a=w_o_a[l], g_q_b=g_q_b[l], g_k_b=g_k_b[l], w_o_b=w_o_b[l],
                  w_out=w_out[l], g_mlp=g_mlp[l], w_up=w_up[l], w_down=w_down[l])
        mp = mixer_inputs(h_p, pos_p, lp)
        h_p = layer_out(h_p, mp, mla_prompt(mp, lp), dsa_prompt(mp), lp)
        lat_p.append(mp['c']); kr_p.append(mp['kr']); k_p.append(mp['kb']); v_p.append(mp['vb']); ik_p.append(mp['ki'])
        ms = mixer_inputs(h_s, pos_s, lp)
        o_a = mla_sample(ms, cache_mla_latent[l], cache_mla_krope[l], page_table, lp)
        o_b = dsa_sample(ms, cache_dsa_k[l], cache_dsa_v[l], cache_idx_k[l], page_table)
        h_s = layer_out(h_s, ms, o_a, o_b, lp)
        lat_s.append(ms['c']); kr_s.append(ms['kr']); k_s.append(ms['kb']); v_s.append(ms['vb']); ik_s.append(ms['ki'])
    return (h_p, h_s,
            jnp.stack(lat_p), jnp.stack(kr_p), jnp.stack(k_p), jnp.stack(v_p), jnp.stack(ik_p),
            jnp.stack(lat_s), jnp.stack(kr_s), jnp.stack(k_s), jnp.stack(v_s), jnp.stack(ik_s))
```

```python
import functools
import math

import jax
import jax.numpy as jnp
import numpy as np
from jax import lax
from jax.experimental import pallas as pl
from jax.experimental.pallas import tpu as pltpu

F32 = jnp.float32
BF16 = jnp.bfloat16
I32 = jnp.int32

D_MODEL = 1024
A_HEADS = 8
A_NOPE = 64
A_ROPE = 32
A_V = 64
A_Q_RANK = 768
A_KV_RANK = 256
A_SCALE = 1.0 / math.sqrt(A_NOPE + A_ROPE)
B_HEADS = 8
B_KV_HEADS = 4
B_HEAD_DIM = 64
B_SCALE = 1.0 / math.sqrt(B_HEAD_DIM)
IDX_HEADS = 8
IDX_DIM = 64
IDX_SCALE = 1.0 / math.sqrt(IDX_DIM)
TOPK_MAX = 256
D_FF = 4 * D_MODEL
ROPE_THETA = 10000.0
EPS = 1e-6
IN_SIZES = (A_Q_RANK, A_KV_RANK, A_ROPE, B_HEADS * B_HEAD_DIM, B_KV_HEADS * B_HEAD_DIM,
            B_KV_HEADS * B_HEAD_DIM, IDX_HEADS * IDX_DIM, IDX_DIM, IDX_HEADS, 2 * D_MODEL)

LANES = 128
HEAD_BLOCK = LANES
NEG = -1e30
INT_MIN = -(2 ** 31)
VMEM_LIMIT = 56 * 1024 * 1024

PROJ_TILE = 256
MLA_TQ = 256
DSA_TQ = 128
DSA_TK = 512
PAGES_PER_STEP = 16
PAGES_PER_CHUNK = 4
T_PAD = 8


def _dot(a, b):
    return jnp.dot(a, b, preferred_element_type=F32)


def _dot_nt(a, b):
    return lax.dot_general(a, b, (((1,), (1,)), ((), ())), preferred_element_type=F32)


def _rms_full(x, g):
    ms = jnp.mean(x * x, axis=-1, keepdims=True)
    return x * lax.rsqrt(ms + EPS) * g


def _split_bf16(x):
    hi = x.astype(BF16)
    lo = (x - hi.astype(F32)).astype(BF16)
    return hi, lo


def _group_rms(x, gs_ref, inv_ref, e_ref):
    hi, lo = _split_bf16(x * x)
    gs = gs_ref[...]
    ssum = _dot(hi, gs) + _dot(lo, gs)
    r = lax.rsqrt(ssum * inv_ref[...] + EPS)
    rhi, rlo = _split_bf16(r)
    e = e_ref[...]
    return x * (_dot(rhi, e) + _dot(rlo, e))


def _rope(x, tabs, half):
    c, sa, sb = tabs
    outs = []
    for j in range(x.shape[1] // LANES):
        xc = x[:, j * LANES:(j + 1) * LANES]
        outs.append(xc * c + pltpu.roll(xc, LANES - half, 1) * sa + pltpu.roll(xc, half, 1) * sb)
    return outs[0] if len(outs) == 1 else jnp.concatenate(outs, axis=1)


def _proj_kernel(x_ref, gattn_ref, ca_ref, saa_ref, sba_ref, cb_ref, sab_ref, sbb_ref,
                 wqa_ref, gqa_ref, wuq_ref, gsqa_ref, invqa_ref, eqa_ref, gainqa_ref,
                 wckv_ref, gkva_ref, wkr_ref, gainkr_ref,
                 wuk_ref, gskn_ref, invkn_ref, ekn_ref, gainkn_ref, wuv_ref,
                 wqb_ref, gsqb_ref, invqb_ref, eqb_ref, gainqb_ref,
                 wkb_ref, gskb_ref, invkb_ref, ekb_ref, gainkb_ref,
                 wvb_ref, wqi_ref, wki_ref, wwi_ref,
                 qa_o, kcat_o, va_o, lat_o, krope_o, qb_o, kb32_o, kb16_o, vb32_o, vb16_o,
                 qi_o, ki32_o, ki16_o, wi_o):
    n = _rms_full(x_ref[...], gattn_ref[...]).astype(BF16)
    tab_a = (ca_ref[...], saa_ref[...], sba_ref[...])
    tab_b = (cb_ref[...], sab_ref[...], sbb_ref[...])

    qlat = _rms_full(_dot(n, wqa_ref[...]), gqa_ref[...]).astype(BF16)
    qa = _group_rms(_dot(qlat, wuq_ref[...]), gsqa_ref, invqa_ref, eqa_ref) * gainqa_ref[...]
    qa_o[...] = _rope(qa, tab_a, A_ROPE // 2).astype(BF16)

    c = _rms_full(_dot(n, wckv_ref[...]), gkva_ref[...])
    lat_o[...] = c
    c16 = c.astype(BF16)
    kr = _dot(n, wkr_ref[...])
    ms = jnp.sum(kr * kr, axis=-1, keepdims=True) * (1.0 / A_ROPE)
    kr = _rope(kr * lax.rsqrt(ms + EPS) * gainkr_ref[...], tab_a, A_ROPE // 2)
    krope_o[...] = kr[:, A_NOPE:A_NOPE + A_ROPE]
    kn = _group_rms(_dot(c16, wuk_ref[...]), gskn_ref, invkn_ref, ekn_ref) * gainkn_ref[...]
    kcat_o[...] = (kn + jnp.concatenate([kr] * A_HEADS, axis=1)).astype(BF16)
    va_o[...] = _dot(c16, wuv_ref[...]).astype(BF16)

    qb = _group_rms(_dot(n, wqb_ref[...]), gsqb_ref, invqb_ref, eqb_ref) * gainqb_ref[...]
    qb_o[...] = _rope(qb, tab_b, B_HEAD_DIM // 2).astype(BF16)
    kb = _group_rms(_dot(n, wkb_ref[...]), gskb_ref, invkb_ref, ekb_ref) * gainkb_ref[...]
    kb = _rope(kb, tab_b, B_HEAD_DIM // 2)
    kb32_o[...] = kb
    kb16_o[...] = kb.astype(BF16)
    vb = _dot(n, wvb_ref[...])
    vb32_o[...] = vb
    vb16_o[...] = vb.astype(BF16)

    qi = _rope(_dot(n, wqi_ref[...]), tab_b, IDX_DIM // 2) * IDX_SCALE
    qi_o[...] = qi.astype(BF16)
    ki = _rope(_dot(n, wki_ref[...]), tab_b, IDX_DIM // 2)
    ki32_o[...] = ki[:, :IDX_DIM]
    ki16_o[...] = ki.astype(BF16)
    wi_o[...] = _dot(n, wwi_ref[...]) * (IDX_HEADS ** -0.5)


def _const_spec(a):
    nd = a.ndim
    return pl.BlockSpec(a.shape, lambda *_: (0,) * nd)


def _group_mats(width, groups):
    gs = np.zeros((width, LANES), np.float32)
    inv = np.ones((1, LANES), np.float32)
    for j, (start, size) in enumerate(groups):
        gs[start:start + size, j] = 1.0
        inv[0, j] = 1.0 / size
    return jnp.asarray(gs, BF16), jnp.asarray(inv), jnp.asarray(gs.T, BF16)


def _b_half(h):
    return (h // (B_HEADS // B_KV_HEADS)) % 2


def _rope_tables(pos):
    posf = pos.astype(F32)[:, None]

    def cs(d):
        inv = ROPE_THETA ** (-jnp.arange(0, d, 2, dtype=F32) / d)
        ang = posf * inv[None, :]
        return jnp.cos(ang), jnp.sin(ang)

    r = pos.shape[0]
    c16, s16 = cs(A_ROPE)
    one, zero = jnp.ones((r, 1), F32), jnp.zeros((r, 1), F32)
    z16 = jnp.zeros((r, A_ROPE // 2), F32)
    ca = jnp.concatenate([jnp.tile(one, (1, A_NOPE)), c16, c16, jnp.tile(one, (1, 32))], axis=1)
    saa = jnp.concatenate([jnp.tile(zero, (1, A_NOPE)), -s16, z16, jnp.tile(zero, (1, 32))], axis=1)
    sba = jnp.concatenate([jnp.tile(zero, (1, A_NOPE)), z16, s16, jnp.tile(zero, (1, 32))], axis=1)
    c32, s32 = cs(B_HEAD_DIM)
    z32 = jnp.zeros_like(s32)
    cb = jnp.concatenate([c32] * 4, axis=1)
    sab = jnp.concatenate([-s32, z32] * 2, axis=1)
    sbb = jnp.concatenate([z32, s32] * 2, axis=1)
    return ca, saa, sba, cb, sab, sbb


def _prep_proj_params(p):
    offs = np.cumsum((0,) + IN_SIZES)
    w_in = p['w_in']
    w = [w_in[:, offs[i]:offs[i + 1]] for i in range(len(IN_SIZES))]
    w_qa, w_ckv, w_kr, w_qb, w_kb, w_vb, w_qi, w_ki, w_wi, w_gate = w
    d = D_MODEL

    def blocks(wm, heads, dim):
        wm = wm.reshape(wm.shape[0], heads, dim)
        return jnp.pad(wm, ((0, 0), (0, 0), (0, HEAD_BLOCK - dim))).reshape(wm.shape[0], heads * HEAD_BLOCK)

    w_uq = blocks(p['w_uq'], A_HEADS, A_NOPE + A_ROPE)
    w_uk_c = p['w_uk'].reshape(A_KV_RANK, A_HEADS * A_NOPE)
    w_uk = blocks(w_uk_c, A_HEADS, A_NOPE)
    w_uv = p['w_uv'].reshape(A_KV_RANK, A_HEADS * A_V)
    w_kr = jnp.pad(w_kr, ((0, 0), (A_NOPE, LANES - A_NOPE - A_ROPE)))
    qb_cols, gqb = [], []
    zc = jnp.zeros((d, B_HEAD_DIM), F32)
    zg = jnp.zeros((B_HEAD_DIM,), F32)
    for h in range(B_HEADS):
        wh = w_qb[:, h * B_HEAD_DIM:(h + 1) * B_HEAD_DIM]
        qb_cols += [zc, wh] if _b_half(h) else [wh, zc]
        gqb += [zg, p['g_q_b']] if _b_half(h) else [p['g_q_b'], zg]
    w_qb = jnp.concatenate(qb_cols, axis=1)
    w_qi = blocks(w_qi, IDX_HEADS, IDX_DIM)
    w_ki = jnp.pad(w_ki, ((0, 0), (0, LANES - IDX_DIM)))
    w_wi = jnp.pad(w_wi, ((0, 0), (0, LANES - IDX_HEADS)))

    z32 = jnp.zeros((HEAD_BLOCK - A_NOPE - A_ROPE,), F32)
    gain_qa = jnp.tile(jnp.concatenate([p['g_qn_a'], p['g_qr_a'], z32]), A_HEADS) * A_SCALE
    gain_kn = jnp.tile(jnp.concatenate([p['g_kn_a'], jnp.zeros((HEAD_BLOCK - A_NOPE,), F32)]), A_HEADS)
    gain_kr = jnp.concatenate([jnp.zeros((A_NOPE,), F32), p['g_kr_a'], z32])
    gain_qb = jnp.concatenate(gqb) * B_SCALE
    gain_kb = jnp.tile(p['g_k_b'], B_KV_HEADS)

    qa_groups = []
    for h in range(A_HEADS):
        qa_groups += [(h * HEAD_BLOCK, A_NOPE), (h * HEAD_BLOCK + A_NOPE, A_ROPE)]
    kn_groups = [(h * HEAD_BLOCK, A_NOPE) for h in range(A_HEADS)]
    qb_groups = [(h * HEAD_BLOCK + B_HEAD_DIM * _b_half(h), B_HEAD_DIM) for h in range(B_HEADS)]
    kb_groups = [(g * B_HEAD_DIM, B_HEAD_DIM) for g in range(B_KV_HEADS)]

    row = lambda v: v.reshape(1, -1).astype(F32)
    b16 = lambda m: m.astype(BF16)
    proj = [row(p['g_attn']),
            b16(w_qa), row(p['g_q_a']), b16(w_uq), *_group_mats(A_HEADS * HEAD_BLOCK, qa_groups), row(gain_qa),
            b16(w_ckv), row(p['g_kv_a']), b16(w_kr), row(gain_kr),
            b16(w_uk), *_group_mats(A_HEADS * HEAD_BLOCK, kn_groups), row(gain_kn), b16(w_uv),
            b16(w_qb), *_group_mats(B_HEADS * HEAD_BLOCK, qb_groups), row(gain_qb),
            b16(w_kb), *_group_mats(B_KV_HEADS * B_HEAD_DIM, kb_groups), row(gain_kb),
            b16(w_vb), b16(w_qi), b16(w_ki), b16(w_wi)]
    extra = dict(w_gate=b16(w_gate), w_uk_blocks=b16(w_uk), w_uk_compact=b16(w_uk_c), w_uv=b16(w_uv),
                 g_kn=row(p['g_kn_a']))
    return proj, extra


def _proj_call(x, tables, proj_params):
    m = x.shape[0]
    t = min(PROJ_TILE, m)
    assert m % t == 0 and tables[0].shape[0] % t == 0
    nt = tables[0].shape[0] // t
    row_spec = lambda w: pl.BlockSpec((t, w), lambda i: (i, 0))
    tab_spec = pl.BlockSpec((t, LANES), lambda i: (i % nt, 0))
    hb = A_HEADS * HEAD_BLOCK
    outs = [(hb, BF16), (hb, BF16), (A_HEADS * A_V, BF16), (A_KV_RANK, F32), (A_ROPE, F32),
            (B_HEADS * HEAD_BLOCK, BF16), (B_KV_HEADS * B_HEAD_DIM, F32), (B_KV_HEADS * B_HEAD_DIM, BF16),
            (B_KV_HEADS * B_HEAD_DIM, F32), (B_KV_HEADS * B_HEAD_DIM, BF16),
            (IDX_HEADS * HEAD_BLOCK, BF16), (IDX_DIM, F32), (LANES, BF16), (LANES, F32)]
    return pl.pallas_call(
        _proj_kernel,
        grid=(m // t,),
        in_specs=[row_spec(D_MODEL), _const_spec(proj_params[0])] + [tab_spec] * 6
                 + [_const_spec(a) for a in proj_params[1:]],
        out_specs=[row_spec(w) for w, _ in outs],
        out_shape=[jax.ShapeDtypeStruct((m, w), dt) for w, dt in outs],
        compiler_params=pltpu.CompilerParams(dimension_semantics=("arbitrary",), vmem_limit_bytes=VMEM_LIMIT),
        name="proj",
    )(x, proj_params[0], *tables, *proj_params[1:])


def _mla_prompt_kernel(q_ref, k_ref, v_ref, o_ref, *, tq):
    i = pl.program_id(2)
    rows = lax.broadcasted_iota(I32, (tq, tq), 0)
    cols = lax.broadcasted_iota(I32, (tq, tq), 1)
    causal = cols <= rows
    lane = lax.broadcasted_iota(I32, (tq, LANES), 1)
    outs = []
    for hh in range(2):
        q = q_ref[:, hh * HEAD_BLOCK:(hh + 1) * HEAD_BLOCK]

        def step(j, carry, masked, hh=hh, q=q):
            m, l, acc = carry
            off = pl.multiple_of(j * tq, tq)
            k = k_ref[pl.ds(off, tq), hh * HEAD_BLOCK:(hh + 1) * HEAD_BLOCK]
            v = v_ref[pl.ds(off, tq), :]
            s = _dot_nt(q, k)
            if masked:
                s = jnp.where(causal, s, NEG)
            m_new = jnp.maximum(m, jnp.max(s, axis=-1, keepdims=True))
            p = jnp.exp(s - m_new)
            corr = jnp.exp(m - m_new)
            l = l * corr + jnp.sum(p, axis=-1, keepdims=True)
            acc = acc * corr + _dot(p.astype(BF16), v)
            return m_new, l, acc

        init = (jnp.full((tq, 1), NEG, F32), jnp.zeros((tq, 1), F32), jnp.zeros((tq, LANES), F32))
        carry = lax.fori_loop(0, i, functools.partial(step, masked=False), init)
        _, l, acc = step(i, carry, masked=True)
        outs.append(acc / l)
    o_ref[...] = jnp.where(lane < A_V, outs[0], outs[1]).astype(o_ref.dtype)


def _mla_prompt_call(qa, kcat, va, n, s):
    tq = min(MLA_TQ, s)
    nq = s // tq
    pairs = A_HEADS // 2
    return pl.pallas_call(
        functools.partial(_mla_prompt_kernel, tq=tq),
        grid=(n, pairs, nq),
        in_specs=[pl.BlockSpec((tq, 2 * HEAD_BLOCK), lambda b, h, i: (b * nq + i, h)),
                  pl.BlockSpec((s, 2 * HEAD_BLOCK), lambda b, h, i: (b, h)),
                  pl.BlockSpec((s, 2 * A_V), lambda b, h, i: (b, h))],
        out_specs=pl.BlockSpec((tq, 2 * A_V), lambda b, h, i: (b * nq + i, h)),
        out_shape=jax.ShapeDtypeStruct((n * s, A_HEADS * A_V), BF16),
        compiler_params=pltpu.CompilerParams(dimension_semantics=("arbitrary",) * 3, vmem_limit_bytes=VMEM_LIMIT),
        name="mla_prompt",
    )(qa, kcat, va)


def _order_key(score):
    bits = pltpu.bitcast(score, I32)
    return bits ^ ((bits >> 31) & 0x7FFFFFFF)


def _select_threshold(count_ge, count_eq_below, shape, nsel, nbits, row_ok):
    c0 = count_ge(jnp.zeros(shape, I32))
    thr = jnp.where(c0 >= nsel, 0, INT_MIN).astype(I32)

    def bit_body(p, thr):
        cand = thr + lax.shift_left(jnp.int32(1), 30 - p)
        return jnp.where(count_ge(cand) >= nsel, cand, thr)

    thr = lax.fori_loop(0, 31, bit_body, thr)
    n_ge = count_ge(thr)
    n_gt = jnp.where(thr == 2 ** 31 - 1, 0, count_ge(thr + 1))
    need = nsel - n_gt
    tie = (n_ge > nsel) & (thr > INT_MIN) & row_ok
    all_cols = jnp.full(shape, 2 ** 30, I32)

    def resolve():
        def jbit(p, jmax):
            cand = jmax | lax.shift_left(jnp.int32(1), nbits - 1 - p)
            return jnp.where(count_eq_below(thr, cand) < need, cand, jmax)
        jmax = lax.fori_loop(0, nbits, jbit, jnp.zeros(shape, I32))
        return jnp.where(tie, jmax, all_cols)

    jmax = lax.cond(jnp.max(tie.astype(I32)) > 0, resolve, lambda: all_cols)
    return thr, jmax


def _lane_fold(x):
    acc = x[:, :LANES]
    for j in range(1, x.shape[1] // LANES):
        acc = acc + x[:, j * LANES:(j + 1) * LANES]
    return acc


def _dsa_prompt_kernel(qi_ref, wi_ref, qb_ref, ki_ref, kb_ref, vb_ref, o_ref, key_scr, *, tq, tk, nsel, nbits):
    i = pl.program_id(1)
    q0 = i * tq
    nch = q0 // tk + 1
    row_g = q0 + lax.broadcasted_iota(I32, (tq, tk), 0)
    col_l = lax.broadcasted_iota(I32, (tq, tk), 1)

    qi = jnp.concatenate([qi_ref[:, h * HEAD_BLOCK:(h + 1) * HEAD_BLOCK] for h in range(IDX_HEADS)], axis=0)
    w = wi_ref[...]
    wcol = [w[:, h:h + 1] for h in range(IDX_HEADS)]

    def idx_chunk(c, carry):
        off = pl.multiple_of(c * tk, tk)
        s = _dot_nt(qi, ki_ref[pl.ds(off, tk), :])
        score = jnp.zeros((tq, tk), F32)
        for h in range(IDX_HEADS):
            score = score + wcol[h] * jnp.maximum(s[h * tq:(h + 1) * tq], 0.0)
        key = jnp.where(off + col_l <= row_g, _order_key(score), INT_MIN)
        key_scr[:, pl.ds(off, tk)] = key
        return carry

    lax.fori_loop(0, nch, idx_chunk, 0)

    def count(pred):
        def body(c, acc):
            off = pl.multiple_of(c * tk, tk)
            hit = pred(key_scr[:, pl.ds(off, tk)], off + col_l)
            return acc + _lane_fold(jnp.where(hit, 1, 0).astype(I32))
        acc = lax.fori_loop(0, nch, body, jnp.zeros((tq, LANES), I32))
        return jnp.sum(acc, axis=-1, keepdims=True)

    count_ge = lambda cand: count(lambda k, col: k >= cand)
    count_eq_below = lambda thr, j: count(lambda k, col: (k == thr) & (col < j))
    thr, jmax = _select_threshold(count_ge, count_eq_below, (tq, 1), nsel, nbits, row_g[:, :1] >= 0)

    hp = B_HEADS // 2
    qs = [jnp.concatenate([qb_ref[:, h * HEAD_BLOCK:(h + 1) * HEAD_BLOCK] for h in range(p * hp, (p + 1) * hp)],
                          axis=0) for p in range(2)]

    def att_chunk(c, carry):
        off = pl.multiple_of(c * tk, tk)
        key = key_scr[:, pl.ds(off, tk)]
        col = off + col_l
        sel = ((key > thr) | ((key == thr) & (col <= jmax))) & (col <= row_g)
        new = []
        for p in range(2):
            m, l, acc = carry[p]
            kc = kb_ref[pl.ds(off, tk), p * LANES:(p + 1) * LANES]
            vc = vb_ref[pl.ds(off, tk), p * LANES:(p + 1) * LANES]
            s = jnp.where(sel[None], _dot_nt(qs[p], kc).reshape(hp, tq, tk), NEG)
            m_new = jnp.maximum(m, jnp.max(s, axis=-1, keepdims=True))
            pe = jnp.where(sel[None], jnp.exp(s - m_new), 0.0)
            corr = jnp.exp(m - m_new)
            l = l * corr + jnp.sum(pe, axis=-1, keepdims=True)
            pv = _dot(pe.reshape(hp * tq, tk).astype(BF16), vc).reshape(hp, tq, LANES)
            new.append((m_new, l, acc * corr + pv))
        return tuple(new)

    init = tuple((jnp.full((hp, tq, 1), NEG, F32), jnp.zeros((hp, tq, 1), F32), jnp.zeros((hp, tq, LANES), F32))
                 for _ in range(2))
    res = lax.fori_loop(0, nch, att_chunk, init)
    o_ref[...] = _assemble_b_heads([res[p][2] / res[p][1] for p in range(2)], tq).astype(o_ref.dtype)


def _assemble_b_heads(accs, rows):
    lane = lax.broadcasted_iota(I32, (rows, LANES), 1)
    swap = lambda a: pltpu.roll(a, B_HEAD_DIM, 1)
    blocks = []
    for p in range(2):
        a = accs[p]
        blocks.append(jnp.where(lane < B_HEAD_DIM, a[0], swap(a[1])))
        blocks.append(jnp.where(lane < B_HEAD_DIM, swap(a[2]), a[3]))
    return jnp.concatenate(blocks, axis=1)


def _dsa_prompt_call(qi, wi, qb, ki16, kb16, vb16, n, s):
    tq = min(DSA_TQ, s)
    tk = min(DSA_TK, s)
    nq = s // tq
    nsel = min(TOPK_MAX, s // 4)
    kvw = B_KV_HEADS * B_HEAD_DIM
    qspec = lambda w: pl.BlockSpec((tq, w), lambda b, i: (b * nq + i, 0))
    kspec = lambda w: pl.BlockSpec((s, w), lambda b, i: (b, 0))
    return pl.pallas_call(
        functools.partial(_dsa_prompt_kernel, tq=tq, tk=tk, nsel=nsel, nbits=int(s).bit_length()),
        grid=(n, nq),
        in_specs=[qspec(IDX_HEADS * HEAD_BLOCK), qspec(LANES), qspec(B_HEADS * HEAD_BLOCK),
                  kspec(LANES), kspec(kvw), kspec(kvw)],
        out_specs=qspec(B_HEADS * B_HEAD_DIM),
        out_shape=jax.ShapeDtypeStruct((n * s, B_HEADS * B_HEAD_DIM), BF16),
        scratch_shapes=[pltpu.VMEM((tq, s), I32)],
        compiler_params=pltpu.CompilerParams(dimension_semantics=("arbitrary",) * 2, vmem_limit_bytes=VMEM_LIMIT),
        name="dsa_prompt",
    )(qi, wi, qb, ki16, kb16, vb16)


def _mla_sample_kernel(pt_ref, q_ref, latn_ref, krn_ref, qmask_ref, wukb_ref, wukc_ref, gst_ref, wuv_ref,
                       hmask_ref, place_ref, *rest, pps, ppc, page, n_tok):
    lat_refs = rest[:pps]
    kr_refs = rest[pps:2 * pps]
    o_ref, at_scr, qr_scr, m_scr, l_scr, acc_scr = rest[2 * pps:]
    j = pl.program_id(1)
    rows = A_HEADS * T_PAD

    @pl.when(j == 0)
    def _():
        q8 = q_ref[0].astype(F32)
        qpad = (jnp.concatenate([q8] * A_HEADS, axis=0) * qmask_ref[...]).astype(BF16)
        at_scr[...] = _dot_nt(qpad, wukb_ref[...]).astype(BF16)
        qr = jnp.concatenate([q8[:, h * HEAD_BLOCK:(h + 1) * HEAD_BLOCK] for h in range(A_HEADS)], axis=0)
        qr_scr[...] = _dot(qr.astype(BF16), place_ref[...]).astype(BF16)
        m_scr[...] = jnp.full((rows, 1), NEG, F32)
        l_scr[...] = jnp.zeros((rows, 1), F32)
        acc_scr[...] = jnp.zeros((rows, A_KV_RANK), F32)

    def attend(c, kr, mask):
        c16 = c.astype(BF16)
        kn = _dot(c16, wukc_ref[...])
        hi, lo = _split_bf16(kn * kn)
        gst = gst_ref[...]
        msq = (_dot_nt(gst, hi) + _dot_nt(gst, lo)) * (1.0 / A_NOPE)
        s = _dot_nt(at_scr[...], c16) * lax.rsqrt(msq + EPS) + _dot_nt(qr_scr[...], kr.astype(BF16))
        if mask is not None:
            s = jnp.where(mask, s, NEG)
        m = m_scr[...]
        m_new = jnp.maximum(m, jnp.max(s, axis=-1, keepdims=True))
        p = jnp.exp(s - m_new)
        if mask is not None:
            p = jnp.where(mask, p, 0.0)
        corr = jnp.exp(m - m_new)
        l_scr[...] = l_scr[...] * corr + jnp.sum(p, axis=-1, keepdims=True)
        acc_scr[...] = acc_scr[...] * corr + _dot(p.astype(BF16), c16)
        m_scr[...] = m_new

    for g in range(pps // ppc):
        c = jnp.concatenate([lat_refs[g * ppc + k][0] for k in range(ppc)], axis=0)
        kr = jnp.concatenate([kr_refs[g * ppc + k][0] for k in range(ppc)], axis=0)
        attend(c, kr, None)

    @pl.when(j == pl.num_programs(1) - 1)
    def _():
        r = lax.broadcasted_iota(I32, (rows, page), 0) & (T_PAD - 1)
        col = lax.broadcasted_iota(I32, (rows, page), 1)
        attend(latn_ref[0], krn_ref[0], (col <= r) & (col < n_tok))
        lat_out = (acc_scr[...] / l_scr[...]).astype(BF16)
        full = _dot(lat_out, wuv_ref[...])
        hm = hmask_ref[...]
        out = jnp.zeros((T_PAD, A_HEADS * A_V), F32)
        for h in range(A_HEADS):
            out = out + full[h * T_PAD:(h + 1) * T_PAD] * hm[h:h + 1]
        o_ref[0] = out.astype(o_ref.dtype)


def _pad_tokens(a, nseq, t, rows=T_PAD):
    a = a.reshape(nseq, t, a.shape[-1])
    return jnp.pad(a, ((0, 0), (0, rows - t), (0, 0)))


def _mla_sample_call(page_table, qa, lat_new, kr_new, cache_lat, cache_kr, extra, t):
    nseq, npages = page_table.shape
    page = cache_lat.shape[1]
    pps = min(PAGES_PER_STEP, npages)
    ppc = min(PAGES_PER_CHUNK, pps)
    assert npages % pps == 0 and pps % ppc == 0
    rows = A_HEADS * T_PAD
    hb = A_HEADS * HEAD_BLOCK

    qmask = np.zeros((rows, hb), np.float32)
    gst = np.zeros((rows, A_HEADS * A_NOPE), np.float32)
    hmask = np.zeros((A_HEADS, A_HEADS * A_V), np.float32)
    for h in range(A_HEADS):
        qmask[h * T_PAD:(h + 1) * T_PAD, h * HEAD_BLOCK:h * HEAD_BLOCK + A_NOPE] = 1.0
        gst[h * T_PAD:(h + 1) * T_PAD, h * A_NOPE:(h + 1) * A_NOPE] = 1.0
        hmask[h, h * A_V:(h + 1) * A_V] = 1.0
    gkn_blocks = jnp.tile(jnp.pad(extra['g_kn'], ((0, 0), (0, HEAD_BLOCK - A_NOPE))), (1, A_HEADS))
    qmask = jnp.asarray(qmask) * gkn_blocks
    place = np.zeros((HEAD_BLOCK, A_ROPE), np.float32)
    place[A_NOPE:A_NOPE + A_ROPE, :] = np.eye(A_ROPE, dtype=np.float32)
    consts = [qmask, extra['w_uk_blocks'], extra['w_uk_compact'], jnp.asarray(gst, BF16), extra['w_uv'],
              jnp.asarray(hmask), jnp.asarray(place, BF16)]

    seq_spec = lambda w, r=T_PAD: pl.BlockSpec((1, r, w), lambda n, j, pt: (n, 0, 0))
    cspec = lambda a: pl.BlockSpec(a.shape, lambda n, j, pt: (0,) * a.ndim)
    page_spec = lambda w, k: pl.BlockSpec((1, page, w), lambda n, j, pt, k=k: (pt[n, j * pps + k], 0, 0))
    grid_spec = pltpu.PrefetchScalarGridSpec(
        num_scalar_prefetch=1,
        grid=(nseq, npages // pps),
        in_specs=[seq_spec(hb), seq_spec(A_KV_RANK, page), seq_spec(A_ROPE, page)] + [cspec(a) for a in consts]
                 + [page_spec(A_KV_RANK, k) for k in range(pps)] + [page_spec(A_ROPE, k) for k in range(pps)],
        out_specs=seq_spec(A_HEADS * A_V),
        scratch_shapes=[pltpu.VMEM((rows, A_KV_RANK), BF16), pltpu.VMEM((rows, A_ROPE), BF16),
                        pltpu.VMEM((rows, 1), F32), pltpu.VMEM((rows, 1), F32), pltpu.VMEM((rows, A_KV_RANK), F32)],
    )
    out = pl.pallas_call(
        functools.partial(_mla_sample_kernel, pps=pps, ppc=ppc, page=page, n_tok=t),
        grid_spec=grid_spec,
        out_shape=jax.ShapeDtypeStruct((nseq, T_PAD, A_HEADS * A_V), BF16),
        compiler_params=pltpu.CompilerParams(dimension_semantics=("arbitrary",) * 2, vmem_limit_bytes=VMEM_LIMIT),
        name="mla_sample",
    )(page_table, _pad_tokens(qa, nseq, t), _pad_tokens(lat_new, nseq, t, page), _pad_tokens(kr_new, nseq, t, page),
      *consts, *([cache_lat] * pps), *([cache_kr] * pps))
    return out[:, :t].reshape(nseq * t, A_HEADS * A_V)


def _dsa_sample_kernel(pt_ref, qi_ref, wi_ref, qb_ref, kin_ref, kbn_ref, vbn_ref, *rest,
                       pps, page, nsel, nbits, n_tok, past):
    ik_refs = rest[:pps]
    k_refs = rest[pps:2 * pps]
    v_refs = rest[2 * pps:3 * pps]
    o_ref, key_scr, thr_scr, jmax_scr, m_scr, l_scr, acc_scr = rest[3 * pps:]
    phase = pl.program_id(1)
    j = pl.program_id(2)
    last = pl.num_programs(2) - 1
    kt = pps * page
    hp = B_HEADS // 2
    tok = lax.broadcasted_iota(I32, (T_PAD, 1), 0)

    def new_visible(width):
        col = lax.broadcasted_iota(I32, (T_PAD, width), 1)
        return (col <= tok) & (col < n_tok)

    @pl.when(phase == 0)
    def _():
        q8 = qi_ref[0].astype(F32)
        qi = jnp.concatenate([q8[:, h * HEAD_BLOCK:h * HEAD_BLOCK + IDX_DIM] for h in range(IDX_HEADS)],
                             axis=0).astype(BF16)
        w = wi_ref[0]

        def score(ik16):
            s = _dot_nt(qi, ik16)
            tot = jnp.zeros((T_PAD, s.shape[1]), F32)
            for h in range(IDX_HEADS):
                tot = tot + w[:, h:h + 1] * jnp.maximum(s[h * T_PAD:(h + 1) * T_PAD], 0.0)
            return _order_key(tot)

        ik = jnp.concatenate([r[0] for r in ik_refs], axis=0).astype(BF16)
        key_scr[:, pl.ds(pl.multiple_of(j * kt, kt), kt)] = score(ik)

        @pl.when(j == last)
        def _():
            knew = score(kin_ref[0][:, :IDX_DIM])
            key_scr[:, past:past + page] = jnp.where(new_visible(page), knew, INT_MIN)
            col = lax.broadcasted_iota(I32, (T_PAD, past + page), 1)

            def count_ge(cand):
                return jnp.sum(jnp.where(key_scr[...] >= cand, 1, 0).astype(I32), axis=-1, keepdims=True)

            def count_eq_below(thr, jj):
                hit = (key_scr[...] == thr) & (col < jj)
                return jnp.sum(jnp.where(hit, 1, 0).astype(I32), axis=-1, keepdims=True)

            thr, jmax = _select_threshold(count_ge, count_eq_below, (T_PAD, 1), nsel, nbits, tok < n_tok)
            thr_scr[...] = thr
            jmax_scr[...] = jmax
            m_scr[...] = jnp.full(m_scr.shape, NEG, F32)
            l_scr[...] = jnp.zeros(l_scr.shape, F32)
            acc_scr[...] = jnp.zeros(acc_scr.shape, F32)

    @pl.when(phase == 1)
    def _():
        q8 = qb_ref[0].astype(F32)
        qs = [jnp.concatenate([q8[:, h * HEAD_BLOCK:(h + 1) * HEAD_BLOCK] for h in range(p * hp, (p + 1) * hp)],
                              axis=0).astype(BF16) for p in range(2)]
        thr = thr_scr[...]
        jmax = jmax_scr[...]

        def attend(k16, v16, key, col0, extra_mask):
            width = key.shape[1]
            col = col0 + lax.broadcasted_iota(I32, (T_PAD, width), 1)
            sel = (key > thr) | ((key == thr) & (col <= jmax))
            if extra_mask is not None:
                sel = sel & extra_mask
            for p in range(2):
                s = _dot_nt(qs[p], k16[:, p * LANES:(p + 1) * LANES]).reshape(hp, T_PAD, width)
                s = jnp.where(sel[None], s, NEG)
                m = m_scr[p]
                m_new = jnp.maximum(m, jnp.max(s, axis=-1, keepdims=True))
                pe = jnp.where(sel[None], jnp.exp(s - m_new), 0.0)
                corr = jnp.exp(m - m_new)
                l_scr[p] = l_scr[p] * corr + jnp.sum(pe, axis=-1, keepdims=True)
                pv = _dot(pe.reshape(hp * T_PAD, width).astype(BF16), v16[:, p * LANES:(p + 1) * LANES])
                acc_scr[p] = acc_scr[p] * corr + pv.reshape(hp, T_PAD, LANES)
                m_scr[p] = m_new

        k16 = jnp.concatenate([r[0] for r in k_refs], axis=0).astype(BF16)
        v16 = jnp.concatenate([r[0] for r in v_refs], axis=0).astype(BF16)
        off = pl.multiple_of(j * kt, kt)
        attend(k16, v16, key_scr[:, pl.ds(off, kt)], off, None)

        @pl.when(j == last)
        def _():
            attend(kbn_ref[0], vbn_ref[0], key_scr[:, past:past + page], past, new_visible(page))
            o_ref[0] = _assemble_b_heads([acc_scr[p] / l_scr[p] for p in range(2)], T_PAD).astype(o_ref.dtype)


def _dsa_sample_call(page_table, qi, wi, qb, ki16, kb16, vb16, cache_ik, cache_k, cache_v, t):
    nseq, npages = page_table.shape
    page = cache_ik.shape[1]
    past = npages * page
    pps = min(PAGES_PER_STEP, npages)
    assert npages % pps == 0 and t <= T_PAD
    nj = npages // pps
    nsel = min(TOPK_MAX, (past + t) // 4)
    kvw = B_KV_HEADS * B_HEAD_DIM
    hp = B_HEADS // 2

    new_page = lambda a: _pad_tokens(a, nseq, t, page)

    seq_spec = lambda r, w: pl.BlockSpec((1, r, w), lambda n, ph, j, pt: (n, 0, 0))
    ik_spec = lambda k: pl.BlockSpec(
        (1, page, IDX_DIM), lambda n, ph, j, pt, k=k: (pt[n, (j * (1 - ph) + (nj - 1) * ph) * pps + k], 0, 0))
    kv_spec = lambda k: pl.BlockSpec((1, page, kvw), lambda n, ph, j, pt, k=k: (pt[n, j * ph * pps + k], 0, 0))
    grid_spec = pltpu.PrefetchScalarGridSpec(
        num_scalar_prefetch=1,
        grid=(nseq, 2, nj),
        in_specs=[seq_spec(T_PAD, IDX_HEADS * HEAD_BLOCK), seq_spec(T_PAD, LANES), seq_spec(T_PAD, B_HEADS * HEAD_BLOCK),
                  seq_spec(page, LANES), seq_spec(page, kvw), seq_spec(page, kvw)]
                 + [ik_spec(k) for k in range(pps)] + [kv_spec(k) for k in range(pps)] * 2,
        out_specs=seq_spec(T_PAD, B_HEADS * B_HEAD_DIM),
        scratch_shapes=[pltpu.VMEM((T_PAD, past + page), I32), pltpu.VMEM((T_PAD, 1), I32), pltpu.VMEM((T_PAD, 1), I32),
                        pltpu.VMEM((2, hp, T_PAD, 1), F32), pltpu.VMEM((2, hp, T_PAD, 1), F32),
                        pltpu.VMEM((2, hp, T_PAD, LANES), F32)],
    )
    out = pl.pallas_call(
        functools.partial(_dsa_sample_kernel, pps=pps, page=page, nsel=nsel, nbits=int(past + page).bit_length(),
                          n_tok=t, past=past),
        grid_spec=grid_spec,
        out_shape=jax.ShapeDtypeStruct((nseq, T_PAD, B_HEADS * B_HEAD_DIM), BF16),
        compiler_params=pltpu.CompilerParams(dimension_semantics=("arbitrary",) * 3, vmem_limit_bytes=VMEM_LIMIT),
        name="dsa_sample",
    )(page_table, _pad_tokens(qi, nseq, t), _pad_tokens(wi, nseq, t), _pad_tokens(qb, nseq, t),
      new_page(ki16), new_page(kb16), new_page(vb16),
      *([cache_ik] * pps), *([cache_k] * pps), *([cache_v] * pps))
    return out[:, :t].reshape(nseq * t, B_HEADS * B_HEAD_DIM)


def _merge_kernel(x_ref, oa_ref, ob_ref, gattn_ref, wgate_ref, woa_ref, wob_ref, wout_ref, h_ref):
    x = x_ref[...]
    n = _rms_full(x, gattn_ref[...]).astype(BF16)
    gate = jax.nn.sigmoid(_dot(n, wgate_ref[...]))
    mix = gate[:, :D_MODEL] * _dot(oa_ref[...], woa_ref[...]) + gate[:, D_MODEL:] * _dot(ob_ref[...], wob_ref[...])
    h_ref[...] = x + _dot(mix.astype(BF16), wout_ref[...])


def _mlp_kernel(h_ref, gmlp_ref, wup_ref, wdown_ref, y_ref):
    h = h_ref[...]
    u = jnp.maximum(_dot(_rms_full(h, gmlp_ref[...]).astype(BF16), wup_ref[...]), 0.0)
    y_ref[...] = h + _dot((u * u).astype(BF16), wdown_ref[...])


def _output_call(x, o_a, o_b, g_attn, w_gate, p):
    m = x.shape[0]
    t = min(PROJ_TILE, m)
    row_spec = lambda w: pl.BlockSpec((t, w), lambda i: (i, 0))
    params = pltpu.CompilerParams(dimension_semantics=("arbitrary",), vmem_limit_bytes=VMEM_LIMIT)
    consts = [g_attn, w_gate, p['w_o_a'].astype(BF16), p['w_o_b'].astype(BF16), p['w_out'].astype(BF16)]
    h = pl.pallas_call(
        _merge_kernel,
        grid=(m // t,),
        in_specs=[row_spec(D_MODEL), row_spec(o_a.shape[1]), row_spec(o_b.shape[1])] + [_const_spec(a) for a in consts],
        out_specs=row_spec(D_MODEL),
        out_shape=jax.ShapeDtypeStruct((m, D_MODEL), F32),
        compiler_params=params,
        name="merge",
    )(x, o_a, o_b, *consts)
    consts = [p['g_mlp'].reshape(1, -1), p['w_up'].astype(BF16), p['w_down'].astype(BF16)]
    return pl.pallas_call(
        _mlp_kernel,
        grid=(m // t,),
        in_specs=[row_spec(D_MODEL)] + [_const_spec(a) for a in consts],
        out_specs=row_spec(D_MODEL),
        out_shape=jax.ShapeDtypeStruct((m, D_MODEL), F32),
        compiler_params=params,
        name="mlp",
    )(h, *consts)


def _layer(h_p, h_s, caches, page_table, p):
    n, s, _ = h_p.shape
    nseq, t, _ = h_s.shape
    past = page_table.shape[1] * caches['lat'].shape[1]
    proj_params, extra = _prep_proj_params(p)
    g_attn = proj_params[0]

    xp = h_p.reshape(n * s, D_MODEL)
    (qa, kcat, va, lat, krope, qb, kb32, kb16, vb32, vb16, qi, ki32, ki16, wi) = _proj_call(
        xp, _rope_tables(jnp.arange(s)), proj_params)
    o_a = _mla_prompt_call(qa, kcat, va, n, s)
    o_b = _dsa_prompt_call(qi, wi, qb, ki16, kb16, vb16, n, s)
    y_p = _output_call(xp, o_a, o_b, g_attn, extra['w_gate'], p).reshape(n, s, D_MODEL)
    new_p = (lat.reshape(n, s, -1), krope.reshape(n, s, -1), kb32.reshape(n, s, B_KV_HEADS, B_HEAD_DIM),
             vb32.reshape(n, s, B_KV_HEADS, B_HEAD_DIM), ki32.reshape(n, s, -1))

    xs = h_s.reshape(nseq * t, D_MODEL)
    pos_s = jnp.tile(past + jnp.arange(t), nseq)
    (qa, kcat, va, lat, krope, qb, kb32, kb16, vb32, vb16, qi, ki32, ki16, wi) = _proj_call(
        xs, _rope_tables(pos_s), proj_params)
    o_a = _mla_sample_call(page_table, qa, lat, krope, caches['lat'], caches['kr'], extra, t)
    kvw = B_KV_HEADS * B_HEAD_DIM
    o_b = _dsa_sample_call(page_table, qi, wi, qb, ki16, kb16, vb16, caches['ik'],
                           caches['k'].reshape(*caches['k'].shape[:2], kvw),
                           caches['v'].reshape(*caches['v'].shape[:2], kvw), t)
    y_s = _output_call(xs, o_a, o_b, g_attn, extra['w_gate'], p).reshape(nseq, t, D_MODEL)
    new_s = (lat.reshape(nseq, t, -1), krope.reshape(nseq, t, -1), kb32.reshape(nseq, t, B_KV_HEADS, B_HEAD_DIM),
             vb32.reshape(nseq, t, B_KV_HEADS, B_HEAD_DIM), ki32.reshape(nseq, t, -1))
    return y_p, y_s, new_p, new_s


def kernel(x_prompt, x_sample, cache_mla_latent, cache_mla_krope, cache_dsa_k, cache_dsa_v, cache_idx_k, page_table, g_attn, w_in, g_q_a, w_uq, g_qn_a, g_qr_a, g_kv_a, g_kr_a, w_uk, g_kn_a, w_uv, w_o_a, g_q_b, g_k_b, w_o_b, w_out, g_mlp, w_up, w_down):
    depth = w_in.shape[0]
    h_p, h_s = x_prompt, x_sample
    new_p, new_s = [], []
    for l in range(depth):
        p = dict(g_attn=g_attn[l], w_in=w_in[l], g_q_a=g_q_a[l], w_uq=w_uq[l], g_qn_a=g_qn_a[l],
                 g_qr_a=g_qr_a[l], g_kv_a=g_kv_a[l], g_kr_a=g_kr_a[l], w_uk=w_uk[l], g_kn_a=g_kn_a[l],
                 w_uv=w_uv[l], w_o_a=w_o_a[l], g_q_b=g_q_b[l], g_k_b=g_k_b[l], w_o_b=w_o_b[l],
                 w_out=w_out[l], g_mlp=g_mlp[l], w_up=w_up[l], w_down=w_down[l])
        caches = dict(lat=cache_mla_latent[l], kr=cache_mla_krope[l], k=cache_dsa_k[l], v=cache_dsa_v[l],
                      ik=cache_idx_k[l])
        h_p, h_s, np_l, ns_l = _layer(h_p, h_s, caches, page_table, p)
        new_p.append(np_l)
        new_s.append(ns_l)
    stack = lambda parts, i: jnp.stack([q[i] for q in parts])
    return (h_p, h_s, *[stack(new_p, i) for i in range(5)], *[stack(new_s, i) for i in range(5)])
```

```python
import functools
import math

import jax
import jax.numpy as jnp
import numpy as np
from jax import lax
from jax.experimental import pallas as pl
from jax.experimental.pallas import tpu as pltpu

F32 = jnp.float32
BF16 = jnp.bfloat16
I32 = jnp.int32

D_MODEL = 1024
A_HEADS = 8
A_NOPE = 64
A_ROPE = 32
A_V = 64
A_Q_RANK = 768
A_KV_RANK = 256
A_SCALE = 1.0 / math.sqrt(A_NOPE + A_ROPE)
B_HEADS = 8
B_KV_HEADS = 4
B_HEAD_DIM = 64
B_SCALE = 1.0 / math.sqrt(B_HEAD_DIM)
IDX_HEADS = 8
IDX_DIM = 64
IDX_SCALE = 1.0 / math.sqrt(IDX_DIM)
TOPK_MAX = 256
D_FF = 4 * D_MODEL
ROPE_THETA = 10000.0
EPS = 1e-6
IN_SIZES = (A_Q_RANK, A_KV_RANK, A_ROPE, B_HEADS * B_HEAD_DIM, B_KV_HEADS * B_HEAD_DIM,
            B_KV_HEADS * B_HEAD_DIM, IDX_HEADS * IDX_DIM, IDX_DIM, IDX_HEADS, 2 * D_MODEL)

LANES = 128
HEAD_BLOCK = LANES
NEG = -1e30
INT_MIN = -(2 ** 31)
VMEM_LIMIT = 56 * 1024 * 1024

LOG2E = math.log2(math.e)

PROJ_TILE = 256
MLA_TQ = 512
DSA_TQ = 256
DSA_TK = 512
COUNT_ROWS = 64
PAGES_PER_STEP = 16
PAGES_PER_CHUNK = 8
T_PAD = 8


def _dot(a, b):
    return jnp.dot(a, b, preferred_element_type=F32)


def _dot_nt(a, b):
    return lax.dot_general(a, b, (((1,), (1,)), ((), ())), preferred_element_type=F32)


def _rms_full(x, g):
    ms = jnp.mean(x * x, axis=-1, keepdims=True)
    return x * lax.rsqrt(ms + EPS) * g


def _split_bf16(x):
    hi = x.astype(BF16)
    lo = (x - hi.astype(F32)).astype(BF16)
    return hi, lo


def _group_rms(x, gs_ref, inv_ref, e_ref):
    hi, lo = _split_bf16(x * x)
    gs = gs_ref[...]
    ssum = _dot(hi, gs) + _dot(lo, gs)
    r = lax.rsqrt(ssum * inv_ref[...] + EPS)
    rhi, rlo = _split_bf16(r)
    e = e_ref[...]
    return x * (_dot(rhi, e) + _dot(rlo, e))


def _rope(x, tabs, half):
    c, sa, sb = tabs
    outs = []
    for j in range(x.shape[1] // LANES):
        xc = x[:, j * LANES:(j + 1) * LANES]
        outs.append(xc * c + pltpu.roll(xc, LANES - half, 1) * sa + pltpu.roll(xc, half, 1) * sb)
    return outs[0] if len(outs) == 1 else jnp.concatenate(outs, axis=1)


def _proj_kernel(x_ref, gattn_ref, ca_ref, saa_ref, sba_ref, cb_ref, sab_ref, sbb_ref,
                 wqa_ref, gqa_ref, wuq_ref, gsqa_ref, invqa_ref, eqa_ref, gainqa_ref,
                 wckv_ref, gkva_ref, wkr_ref, gainkr_ref,
                 wuk_ref, gskn_ref, invkn_ref, ekn_ref, gainkn_ref, wuv_ref,
                 wqb_ref, gsqb_ref, invqb_ref, eqb_ref, gainqb_ref,
                 wkb_ref, gskb_ref, invkb_ref, ekb_ref, gainkb_ref,
                 wvb_ref, wqi_ref, wki_ref, wwi_ref,
                 qa_o, kcat_o, va_o, lat_o, krope_o, qb_o, kb32_o, kb16_o, vb32_o, vb16_o,
                 qi_o, ki32_o, ki16_o, wi_o):
    n = _rms_full(x_ref[...], gattn_ref[...]).astype(BF16)
    tab_a = (ca_ref[...], saa_ref[...], sba_ref[...])
    tab_b = (cb_ref[...], sab_ref[...], sbb_ref[...])

    qlat = _rms_full(_dot(n, wqa_ref[...]), gqa_ref[...]).astype(BF16)
    qa = _group_rms(_dot(qlat, wuq_ref[...]), gsqa_ref, invqa_ref, eqa_ref) * gainqa_ref[...]
    qa_o[...] = _rope(qa, tab_a, A_ROPE // 2).astype(BF16)

    c = _rms_full(_dot(n, wckv_ref[...]), gkva_ref[...])
    lat_o[...] = c
    c16 = c.astype(BF16)
    kr = _dot(n, wkr_ref[...])
    ms = jnp.sum(kr * kr, axis=-1, keepdims=True) * (1.0 / A_ROPE)
    kr = _rope(kr * lax.rsqrt(ms + EPS) * gainkr_ref[...], tab_a, A_ROPE // 2)
    krope_o[...] = kr[:, A_NOPE:A_NOPE + A_ROPE]
    kn = _group_rms(_dot(c16, wuk_ref[...]), gskn_ref, invkn_ref, ekn_ref) * gainkn_ref[...]
    kcat_o[...] = (kn + jnp.concatenate([kr] * A_HEADS, axis=1)).astype(BF16)
    va_o[...] = _dot(c16, wuv_ref[...]).astype(BF16)

    qb = _group_rms(_dot(n, wqb_ref[...]), gsqb_ref, invqb_ref, eqb_ref) * gainqb_ref[...]
    qb_o[...] = _rope(qb, tab_b, B_HEAD_DIM // 2).astype(BF16)
    kb = _group_rms(_dot(n, wkb_ref[...]), gskb_ref, invkb_ref, ekb_ref) * gainkb_ref[...]
    kb = _rope(kb, tab_b, B_HEAD_DIM // 2)
    kb32_o[...] = kb
    kb16_o[...] = kb.astype(BF16)
    vb = _dot(n, wvb_ref[...])
    vb32_o[...] = vb
    vb16_o[...] = vb.astype(BF16)

    qi = _rope(_dot(n, wqi_ref[...]), tab_b, IDX_DIM // 2) * IDX_SCALE
    qi_o[...] = qi.astype(BF16)
    ki = _rope(_dot(n, wki_ref[...]), tab_b, IDX_DIM // 2)
    ki32_o[...] = ki[:, :IDX_DIM]
    ki16_o[...] = ki.astype(BF16)
    wi_o[...] = _dot(n, wwi_ref[...]) * (IDX_HEADS ** -0.5)


def _const_spec(a):
    nd = a.ndim
    return pl.BlockSpec(a.shape, lambda *_: (0,) * nd)


def _group_mats(width, groups):
    gs = np.zeros((width, LANES), np.float32)
    inv = np.ones((1, LANES), np.float32)
    for j, (start, size) in enumerate(groups):
        gs[start:start + size, j] = 1.0
        inv[0, j] = 1.0 / size
    return jnp.asarray(gs, BF16), jnp.asarray(inv), jnp.asarray(gs.T, BF16)


def _b_half(h):
    return (h // (B_HEADS // B_KV_HEADS)) % 2


def _rope_tables(pos):
    posf = pos.astype(F32)[:, None]

    def cs(d):
        inv = ROPE_THETA ** (-jnp.arange(0, d, 2, dtype=F32) / d)
        ang = posf * inv[None, :]
        return jnp.cos(ang), jnp.sin(ang)

    r = pos.shape[0]
    c16, s16 = cs(A_ROPE)
    one, zero = jnp.ones((r, 1), F32), jnp.zeros((r, 1), F32)
    z16 = jnp.zeros((r, A_ROPE // 2), F32)
    ca = jnp.concatenate([jnp.tile(one, (1, A_NOPE)), c16, c16, jnp.tile(one, (1, 32))], axis=1)
    saa = jnp.concatenate([jnp.tile(zero, (1, A_NOPE)), -s16, z16, jnp.tile(zero, (1, 32))], axis=1)
    sba = jnp.concatenate([jnp.tile(zero, (1, A_NOPE)), z16, s16, jnp.tile(zero, (1, 32))], axis=1)
    c32, s32 = cs(B_HEAD_DIM)
    z32 = jnp.zeros_like(s32)
    cb = jnp.concatenate([c32] * 4, axis=1)
    sab = jnp.concatenate([-s32, z32] * 2, axis=1)
    sbb = jnp.concatenate([z32, s32] * 2, axis=1)
    return ca, saa, sba, cb, sab, sbb


def _prep_proj_params(p):
    offs = np.cumsum((0,) + IN_SIZES)
    w_in = p['w_in']
    w = [w_in[:, offs[i]:offs[i + 1]] for i in range(len(IN_SIZES))]
    w_qa, w_ckv, w_kr, w_qb, w_kb, w_vb, w_qi, w_ki, w_wi, w_gate = w
    d = D_MODEL

    def blocks(wm, heads, dim):
        wm = wm.reshape(wm.shape[0], heads, dim)
        return jnp.pad(wm, ((0, 0), (0, 0), (0, HEAD_BLOCK - dim))).reshape(wm.shape[0], heads * HEAD_BLOCK)

    w_uq = blocks(p['w_uq'], A_HEADS, A_NOPE + A_ROPE)
    w_uk_c = p['w_uk'].reshape(A_KV_RANK, A_HEADS * A_NOPE)
    w_uk = blocks(w_uk_c, A_HEADS, A_NOPE)
    w_uv = p['w_uv'].reshape(A_KV_RANK, A_HEADS * A_V)
    w_kr = jnp.pad(w_kr, ((0, 0), (A_NOPE, LANES - A_NOPE - A_ROPE)))
    qb_cols, gqb = [], []
    zc = jnp.zeros((d, B_HEAD_DIM), F32)
    zg = jnp.zeros((B_HEAD_DIM,), F32)
    for h in range(B_HEADS):
        wh = w_qb[:, h * B_HEAD_DIM:(h + 1) * B_HEAD_DIM]
        qb_cols += [zc, wh] if _b_half(h) else [wh, zc]
        gqb += [zg, p['g_q_b']] if _b_half(h) else [p['g_q_b'], zg]
    w_qb = jnp.concatenate(qb_cols, axis=1)
    w_qi = blocks(w_qi, IDX_HEADS, IDX_DIM)
    w_ki = jnp.pad(w_ki, ((0, 0), (0, LANES - IDX_DIM)))
    w_wi = jnp.pad(w_wi, ((0, 0), (0, LANES - IDX_HEADS)))

    z32 = jnp.zeros((HEAD_BLOCK - A_NOPE - A_ROPE,), F32)
    gain_qa = jnp.tile(jnp.concatenate([p['g_qn_a'], p['g_qr_a'], z32]), A_HEADS) * (A_SCALE * LOG2E)
    gain_kn = jnp.tile(jnp.concatenate([p['g_kn_a'], jnp.zeros((HEAD_BLOCK - A_NOPE,), F32)]), A_HEADS)
    gain_kr = jnp.concatenate([jnp.zeros((A_NOPE,), F32), p['g_kr_a'], z32])
    gain_qb = jnp.concatenate(gqb) * (B_SCALE * LOG2E)
    gain_kb = jnp.tile(p['g_k_b'], B_KV_HEADS)

    qa_groups = []
    for h in range(A_HEADS):
        qa_groups += [(h * HEAD_BLOCK, A_NOPE), (h * HEAD_BLOCK + A_NOPE, A_ROPE)]
    kn_groups = [(h * HEAD_BLOCK, A_NOPE) for h in range(A_HEADS)]
    qb_groups = [(h * HEAD_BLOCK + B_HEAD_DIM * _b_half(h), B_HEAD_DIM) for h in range(B_HEADS)]
    kb_groups = [(g * B_HEAD_DIM, B_HEAD_DIM) for g in range(B_KV_HEADS)]

    row = lambda v: v.reshape(1, -1).astype(F32)
    b16 = lambda m: m.astype(BF16)
    proj = [row(p['g_attn']),
            b16(w_qa), row(p['g_q_a']), b16(w_uq), *_group_mats(A_HEADS * HEAD_BLOCK, qa_groups), row(gain_qa),
            b16(w_ckv), row(p['g_kv_a']), b16(w_kr), row(gain_kr),
            b16(w_uk), *_group_mats(A_HEADS * HEAD_BLOCK, kn_groups), row(gain_kn), b16(w_uv),
            b16(w_qb), *_group_mats(B_HEADS * HEAD_BLOCK, qb_groups), row(gain_qb),
            b16(w_kb), *_group_mats(B_KV_HEADS * B_HEAD_DIM, kb_groups), row(gain_kb),
            b16(w_vb), b16(w_qi), b16(w_ki), b16(w_wi)]
    extra = dict(w_gate=b16(w_gate), w_uk_blocks=b16(w_uk), w_uk_compact=b16(w_uk_c), w_uv=b16(w_uv),
                 g_kn=row(p['g_kn_a']))
    return proj, extra


def _proj_call(x, tables, proj_params):
    m = x.shape[0]
    t = min(PROJ_TILE, m)
    assert m % t == 0 and tables[0].shape[0] % t == 0
    nt = tables[0].shape[0] // t
    row_spec = lambda w: pl.BlockSpec((t, w), lambda i: (i, 0))
    tab_spec = pl.BlockSpec((t, LANES), lambda i: (i % nt, 0))
    hb = A_HEADS * HEAD_BLOCK
    outs = [(hb, BF16), (hb, BF16), (A_HEADS * A_V, BF16), (A_KV_RANK, F32), (A_ROPE, F32),
            (B_HEADS * HEAD_BLOCK, BF16), (B_KV_HEADS * B_HEAD_DIM, F32), (B_KV_HEADS * B_HEAD_DIM, BF16),
            (B_KV_HEADS * B_HEAD_DIM, F32), (B_KV_HEADS * B_HEAD_DIM, BF16),
            (IDX_HEADS * HEAD_BLOCK, BF16), (IDX_DIM, F32), (LANES, BF16), (LANES, F32)]
    return pl.pallas_call(
        _proj_kernel,
        grid=(m // t,),
        in_specs=[row_spec(D_MODEL), _const_spec(proj_params[0])] + [tab_spec] * 6
                 + [_const_spec(a) for a in proj_params[1:]],
        out_specs=[row_spec(w) for w, _ in outs],
        out_shape=[jax.ShapeDtypeStruct((m, w), dt) for w, dt in outs],
        compiler_params=pltpu.CompilerParams(dimension_semantics=("arbitrary",), vmem_limit_bytes=VMEM_LIMIT),
        name="proj",
    )(x, proj_params[0], *tables, *proj_params[1:])


def _mla_prompt_kernel(q_ref, k_ref, v_ref, o_ref, *, tq):
    i = pl.program_id(2)
    rows = lax.broadcasted_iota(I32, (tq, tq), 0)
    cols = lax.broadcasted_iota(I32, (tq, tq), 1)
    causal = cols <= rows
    lane = lax.broadcasted_iota(I32, (tq, LANES), 1)
    qs = [q_ref[:, hh * HEAD_BLOCK:(hh + 1) * HEAD_BLOCK] for hh in range(2)]

    def step(j, carry, masked):
        off = pl.multiple_of(j * tq, tq)
        v = v_ref[pl.ds(off, tq), :]
        new = []
        for hh in range(2):
            m, l, acc = carry[hh]
            s = _dot_nt(qs[hh], k_ref[pl.ds(off, tq), hh * HEAD_BLOCK:(hh + 1) * HEAD_BLOCK])
            if masked:
                s = jnp.where(causal, s, NEG)
            m_new = jnp.maximum(m, jnp.max(s, axis=-1, keepdims=True))
            p = jnp.exp2(s - m_new)
            corr = jnp.exp2(m - m_new)
            l = l * corr + jnp.sum(p, axis=-1, keepdims=True)
            new.append((m_new, l, acc * corr + _dot(p.astype(BF16), v)))
        return tuple(new)

    init = tuple((jnp.full((tq, 1), NEG, F32), jnp.zeros((tq, 1), F32), jnp.zeros((tq, LANES), F32))
                 for _ in range(2))
    carry = lax.fori_loop(0, i, functools.partial(step, masked=False), init)
    res = step(i, carry, masked=True)
    outs = [res[hh][2] / res[hh][1] for hh in range(2)]
    o_ref[...] = jnp.where(lane < A_V, outs[0], outs[1]).astype(o_ref.dtype)


def _mla_prompt_call(qa, kcat, va, n, s):
    tq = min(MLA_TQ, s)
    nq = s // tq
    pairs = A_HEADS // 2
    return pl.pallas_call(
        functools.partial(_mla_prompt_kernel, tq=tq),
        grid=(n, pairs, nq),
        in_specs=[pl.BlockSpec((tq, 2 * HEAD_BLOCK), lambda b, h, i: (b * nq + i, h)),
                  pl.BlockSpec((s, 2 * HEAD_BLOCK), lambda b, h, i: (b, h)),
                  pl.BlockSpec((s, 2 * A_V), lambda b, h, i: (b, h))],
        out_specs=pl.BlockSpec((tq, 2 * A_V), lambda b, h, i: (b * nq + i, h)),
        out_shape=jax.ShapeDtypeStruct((n * s, A_HEADS * A_V), BF16),
        compiler_params=pltpu.CompilerParams(dimension_semantics=("arbitrary",) * 3, vmem_limit_bytes=VMEM_LIMIT),
        name="mla_prompt",
    )(qa, kcat, va)


def _order_key(score):
    bits = pltpu.bitcast(score, I32)
    return bits ^ ((bits >> 31) & 0x7FFFFFFF)


def _select_threshold(count_ge, count_eq_below, shape, n_all, nsel, nbits, row_ok):
    c0 = count_ge(jnp.zeros(shape, I32))
    thr = jnp.where(c0 >= nsel, 0, INT_MIN).astype(I32)
    n_ge = jnp.where(c0 >= nsel, c0, n_all).astype(I32)

    def bit_body(p, carry):
        thr, n_ge = carry
        cand = thr + lax.shift_left(jnp.int32(1), 30 - p)
        cnt = count_ge(cand)
        ok = cnt >= nsel
        return jnp.where(ok, cand, thr), jnp.where(ok, cnt, n_ge)

    thr, n_ge = lax.fori_loop(0, 31, bit_body, (thr, n_ge))
    tie = (n_ge > nsel) & (thr > INT_MIN) & row_ok
    all_cols = jnp.full(shape, 2 ** 30, I32)

    def resolve():
        need = nsel - jnp.where(thr == 2 ** 31 - 1, 0, count_ge(thr + 1))

        def jbit(p, jmax):
            cand = jmax | lax.shift_left(jnp.int32(1), nbits - 1 - p)
            return jnp.where(count_eq_below(thr, cand) < need, cand, jmax)
        jmax = lax.fori_loop(0, nbits, jbit, jnp.zeros(shape, I32))
        return jnp.where(tie, jmax, all_cols)

    jmax = lax.cond(jnp.max(tie.astype(I32)) > 0, resolve, lambda: all_cols)
    return thr, jmax


def _dsa_prompt_kernel(qi_ref, wi_ref, qb_ref, ki_ref, kb_ref, vb_ref, o_ref, key_scr, *, tq, tk, nsel, nbits):
    i = pl.program_id(1)
    q0 = i * tq
    nch = q0 // tk + 1
    row_g = q0 + lax.broadcasted_iota(I32, (tq, tk), 0)
    col_l = lax.broadcasted_iota(I32, (tq, tk), 1)

    qi = jnp.concatenate([qi_ref[:, h * HEAD_BLOCK:(h + 1) * HEAD_BLOCK] for h in range(IDX_HEADS)], axis=0)
    w = wi_ref[...]
    wcol = [w[:, h:h + 1] for h in range(IDX_HEADS)]

    def idx_chunk(c, carry):
        off = pl.multiple_of(c * tk, tk)
        s = _dot_nt(qi, ki_ref[pl.ds(off, tk), :])
        score = jnp.zeros((tq, tk), F32)
        for h in range(IDX_HEADS):
            score = score + wcol[h] * jnp.maximum(s[h * tq:(h + 1) * tq], 0.0)
        key = jnp.where(off + col_l <= row_g, _order_key(score), INT_MIN)
        key_scr[:, pl.ds(off, tk)] = key
        return carry

    lax.fori_loop(0, nch, idx_chunk, 0)

    rb = min(COUNT_ROWS, tq)
    lane_l = lax.broadcasted_iota(I32, (rb, LANES), 1)

    def count(pred):
        def body(c, accs):
            off = pl.multiple_of(c * tk, tk)
            new = []
            for b in range(tq // rb):
                sl = slice(b * rb, (b + 1) * rb)
                acc = accs[b]
                for jj in range(tk // LANES):
                    k = key_scr[sl, pl.ds(off + jj * LANES, LANES)]
                    acc = acc + jnp.where(pred(k, off + jj * LANES + lane_l, sl), 1, 0).astype(I32)
                new.append(acc)
            return tuple(new)
        accs = lax.fori_loop(0, nch, body, tuple(jnp.zeros((rb, LANES), I32) for _ in range(tq // rb)))
        tot = jnp.sum(jnp.concatenate(accs, axis=0), axis=-1, keepdims=True)
        return jnp.broadcast_to(tot, (tq, LANES))

    count_ge = lambda cand: count(lambda k, col, sl: k >= cand[sl])
    count_eq_below = lambda thr, j: count(lambda k, col, sl: (k == thr[sl]) & (col < j[sl]))
    thr, jmax = _select_threshold(count_ge, count_eq_below, (tq, LANES), nch * tk, nsel, nbits,
                                  lax.broadcasted_iota(I32, (tq, LANES), 0) >= 0)
    thr, jmax = thr[:, :1], jmax[:, :1]

    hp = B_HEADS // 2
    qs = [jnp.concatenate([qb_ref[:, h * HEAD_BLOCK:(h + 1) * HEAD_BLOCK] for h in range(p * hp, (p + 1) * hp)],
                          axis=0) for p in range(2)]

    def att_chunk(c, carry):
        off = pl.multiple_of(c * tk, tk)
        key = key_scr[:, pl.ds(off, tk)]
        col = off + col_l
        sel = ((key > thr) | ((key == thr) & (col <= jmax))) & (col <= row_g)
        new = []
        for p in range(2):
            m, l, acc = carry[p]
            kc = kb_ref[pl.ds(off, tk), p * LANES:(p + 1) * LANES]
            vc = vb_ref[pl.ds(off, tk), p * LANES:(p + 1) * LANES]
            s = jnp.where(sel[None], _dot_nt(qs[p], kc).reshape(hp, tq, tk), NEG)
            m_new = jnp.maximum(m, jnp.max(s, axis=-1, keepdims=True))
            pe = jnp.exp2(s - m_new)
            corr = jnp.exp2(m - m_new)
            l = l * corr + jnp.sum(pe, axis=-1, keepdims=True)
            pv = _dot(pe.reshape(hp * tq, tk).astype(BF16), vc).reshape(hp, tq, LANES)
            new.append((m_new, l, acc * corr + pv))
        return tuple(new)

    init = tuple((jnp.full((hp, tq, 1), NEG, F32), jnp.zeros((hp, tq, 1), F32), jnp.zeros((hp, tq, LANES), F32))
                 for _ in range(2))
    res = lax.fori_loop(0, nch, att_chunk, init)
    o_ref[...] = _assemble_b_heads([res[p][2] / res[p][1] for p in range(2)], tq).astype(o_ref.dtype)


def _assemble_b_heads(accs, rows):
    lane = lax.broadcasted_iota(I32, (rows, LANES), 1)
    swap = lambda a: pltpu.roll(a, B_HEAD_DIM, 1)
    blocks = []
    for p in range(2):
        a = accs[p]
        blocks.append(jnp.where(lane < B_HEAD_DIM, a[0], swap(a[1])))
        blocks.append(jnp.where(lane < B_HEAD_DIM, swap(a[2]), a[3]))
    return jnp.concatenate(blocks, axis=1)


def _dsa_prompt_call(qi, wi, qb, ki16, kb16, vb16, n, s):
    tq = min(DSA_TQ, s)
    tk = min(DSA_TK, s)
    nq = s // tq
    nsel = min(TOPK_MAX, s // 4)
    kvw = B_KV_HEADS * B_HEAD_DIM
    qspec = lambda w: pl.BlockSpec((tq, w), lambda b, i: (b * nq + i, 0))
    kspec = lambda w: pl.BlockSpec((s, w), lambda b, i: (b, 0))
    return pl.pallas_call(
        functools.partial(_dsa_prompt_kernel, tq=tq, tk=tk, nsel=nsel, nbits=int(s).bit_length()),
        grid=(n, nq),
        in_specs=[qspec(IDX_HEADS * HEAD_BLOCK), qspec(LANES), qspec(B_HEADS * HEAD_BLOCK),
                  kspec(LANES), kspec(kvw), kspec(kvw)],
        out_specs=qspec(B_HEADS * B_HEAD_DIM),
        out_shape=jax.ShapeDtypeStruct((n * s, B_HEADS * B_HEAD_DIM), BF16),
        scratch_shapes=[pltpu.VMEM((tq, s), I32)],
        compiler_params=pltpu.CompilerParams(dimension_semantics=("arbitrary",) * 2, vmem_limit_bytes=VMEM_LIMIT),
        name="dsa_prompt",
    )(qi, wi, qb, ki16, kb16, vb16)


def _mla_sample_kernel(pt_ref, q_ref, latn_ref, krn_ref, qmask_ref, wukb_ref, wukt_ref, wuv_ref,
                       hmask_ref, place_ref, *rest, pps, ppc, page, n_tok):
    lat_refs = rest[:pps]
    kr_refs = rest[pps:2 * pps]
    o_ref, lhs_scr, qr_scr, m_scr, l_scr, acc_scr = rest[2 * pps:]
    j = pl.program_id(1)
    rows = A_HEADS * T_PAD
    knw = A_HEADS * A_NOPE

    @pl.when(j == 0)
    def _():
        q8 = q_ref[0].astype(F32)
        qpad = (jnp.concatenate([q8] * A_HEADS, axis=0) * qmask_ref[...]).astype(BF16)
        lhs_scr[:knw, :] = wukt_ref[...]
        lhs_scr[knw:, :] = _dot_nt(qpad, wukb_ref[...]).astype(BF16)
        qr = jnp.concatenate([q8[:, h * HEAD_BLOCK:(h + 1) * HEAD_BLOCK] for h in range(A_HEADS)], axis=0)
        qr_scr[...] = _dot(qr.astype(BF16), place_ref[...]).astype(BF16)
        m_scr[...] = jnp.full((rows, 1), NEG, F32)
        l_scr[...] = jnp.zeros((rows, 1), F32)
        acc_scr[...] = jnp.zeros((rows, A_KV_RANK), F32)

    def partial(c, krt, mask):
        nk = c.shape[0]
        c16 = c.astype(BF16)
        both = _dot_nt(lhs_scr[...], c16)
        knt = both[:knw]
        msq = jnp.sum((knt * knt).reshape(A_HEADS, A_NOPE, nk), axis=1) * (1.0 / A_NOPE)
        r = lax.rsqrt(msq + EPS)
        r = jnp.broadcast_to(r[:, None, :], (A_HEADS, T_PAD, nk)).reshape(rows, nk)
        s = both[knw:] * r + _dot(qr_scr[...], krt.astype(BF16))
        if mask is not None:
            s = jnp.where(mask, s, NEG)
        mg = jnp.max(s, axis=-1, keepdims=True)
        p = jnp.exp2(s - mg)
        return mg, jnp.sum(p, axis=-1, keepdims=True), _dot(p.astype(BF16), c16)

    def merge(parts):
        m = m_scr[...]
        m_new = m
        for mg, _, _ in parts:
            m_new = jnp.maximum(m_new, mg)
        corr = jnp.exp2(m - m_new)
        l = l_scr[...] * corr
        acc = acc_scr[...] * corr
        for mg, lg, pv in parts:
            w = jnp.exp2(mg - m_new)
            l = l + lg * w
            acc = acc + pv * w
        m_scr[...] = m_new
        l_scr[...] = l
        acc_scr[...] = acc

    parts = []
    for g in range(pps // ppc):
        c = jnp.concatenate([lat_refs[g * ppc + k][0] for k in range(ppc)], axis=0)
        krt = jnp.concatenate([kr_refs[g * ppc + k][0] for k in range(ppc)], axis=1)
        parts.append(partial(c, krt, None))
    merge(parts)

    @pl.when(j == pl.num_programs(1) - 1)
    def _():
        r = lax.broadcasted_iota(I32, (rows, page), 0) & (T_PAD - 1)
        col = lax.broadcasted_iota(I32, (rows, page), 1)
        merge([partial(latn_ref[0], krn_ref[0], (col <= r) & (col < n_tok))])
        lat_out = (acc_scr[...] / l_scr[...]).astype(BF16)
        full = _dot(lat_out, wuv_ref[...])
        hm = hmask_ref[...]
        out = jnp.zeros((T_PAD, A_HEADS * A_V), F32)
        for h in range(A_HEADS):
            out = out + full[h * T_PAD:(h + 1) * T_PAD] * hm[h:h + 1]
        o_ref[0] = out.astype(o_ref.dtype)


def _pad_tokens(a, nseq, t, rows=T_PAD):
    a = a.reshape(nseq, t, a.shape[-1])
    return jnp.pad(a, ((0, 0), (0, rows - t), (0, 0)))


def _new_page_t(a, nseq, t, page):
    return jnp.swapaxes(_pad_tokens(a, nseq, t, page), 1, 2)


def _pages_t(cache):
    pool, page = cache.shape[:2]
    return jnp.moveaxis(cache, 1, -1).reshape(pool, -1, page)


def _mla_sample_call(page_table, qa, lat_new, kr_new, cache_lat, cache_kr_t, extra, t):
    nseq, npages = page_table.shape
    page = cache_lat.shape[1]
    pps = min(PAGES_PER_STEP, npages)
    ppc = min(PAGES_PER_CHUNK, pps)
    assert npages % pps == 0 and pps % ppc == 0
    rows = A_HEADS * T_PAD
    hb = A_HEADS * HEAD_BLOCK

    qmask = np.zeros((rows, hb), np.float32)
    hmask = np.zeros((A_HEADS, A_HEADS * A_V), np.float32)
    for h in range(A_HEADS):
        qmask[h * T_PAD:(h + 1) * T_PAD, h * HEAD_BLOCK:h * HEAD_BLOCK + A_NOPE] = 1.0
        hmask[h, h * A_V:(h + 1) * A_V] = 1.0
    gkn_blocks = jnp.tile(jnp.pad(extra['g_kn'], ((0, 0), (0, HEAD_BLOCK - A_NOPE))), (1, A_HEADS))
    qmask = jnp.asarray(qmask) * gkn_blocks
    place = np.zeros((HEAD_BLOCK, A_ROPE), np.float32)
    place[A_NOPE:A_NOPE + A_ROPE, :] = np.eye(A_ROPE, dtype=np.float32)
    consts = [qmask, extra['w_uk_blocks'], extra['w_uk_compact'].T, extra['w_uv'],
              jnp.asarray(hmask), jnp.asarray(place, BF16)]

    seq_spec = lambda r, w: pl.BlockSpec((1, r, w), lambda n, j, pt: (n, 0, 0))
    cspec = lambda a: pl.BlockSpec(a.shape, lambda n, j, pt: (0,) * a.ndim)
    page_spec = lambda r, w, k: pl.BlockSpec((1, r, w), lambda n, j, pt, k=k: (pt[n, j * pps + k], 0, 0))
    grid_spec = pltpu.PrefetchScalarGridSpec(
        num_scalar_prefetch=1,
        grid=(nseq, npages // pps),
        in_specs=[seq_spec(T_PAD, hb), seq_spec(page, A_KV_RANK), seq_spec(A_ROPE, page)] + [cspec(a) for a in consts]
                 + [page_spec(page, A_KV_RANK, k) for k in range(pps)]
                 + [page_spec(A_ROPE, page, k) for k in range(pps)],
        out_specs=seq_spec(T_PAD, A_HEADS * A_V),
        scratch_shapes=[pltpu.VMEM((A_HEADS * A_NOPE + rows, A_KV_RANK), BF16), pltpu.VMEM((rows, A_ROPE), BF16),
                        pltpu.VMEM((rows, 1), F32), pltpu.VMEM((rows, 1), F32), pltpu.VMEM((rows, A_KV_RANK), F32)],
    )
    out = pl.pallas_call(
        functools.partial(_mla_sample_kernel, pps=pps, ppc=ppc, page=page, n_tok=t),
        grid_spec=grid_spec,
        out_shape=jax.ShapeDtypeStruct((nseq, T_PAD, A_HEADS * A_V), BF16),
        compiler_params=pltpu.CompilerParams(dimension_semantics=("arbitrary",) * 2, vmem_limit_bytes=VMEM_LIMIT),
        name="mla_sample",
    )(page_table, _pad_tokens(qa, nseq, t), _pad_tokens(lat_new, nseq, t, page), _new_page_t(kr_new, nseq, t, page),
      *consts, *([cache_lat] * pps), *([cache_kr_t] * pps))
    return out[:, :t].reshape(nseq * t, A_HEADS * A_V)


def _dsa_sample_kernel(pt_ref, qi_ref, wi_ref, qb_ref, kin_ref, kbn_ref, vbn_ref, *rest,
                       pps, page, nsel, nbits, n_tok, past):
    ik_refs = rest[:pps]
    k_refs = rest[pps:2 * pps]
    v_refs = rest[2 * pps:3 * pps]
    o_ref, key_scr, thr_scr, jmax_scr, m_scr, l_scr, acc_scr = rest[3 * pps:]
    phase = pl.program_id(1)
    j = pl.program_id(2)
    last = pl.num_programs(2) - 1
    kt = pps * page
    hp = B_HEADS // 2
    tok = lax.broadcasted_iota(I32, (T_PAD, 1), 0)

    def new_visible(width):
        col = lax.broadcasted_iota(I32, (T_PAD, width), 1)
        return (col <= tok) & (col < n_tok)

    @pl.when(phase == 0)
    def _():
        q8 = qi_ref[0].astype(F32)
        qi = jnp.concatenate([q8[:, h * HEAD_BLOCK:h * HEAD_BLOCK + IDX_DIM] for h in range(IDX_HEADS)],
                             axis=0).astype(BF16)
        w = wi_ref[0]

        def score(ikt16):
            s = _dot(qi, ikt16)
            tot = jnp.zeros((T_PAD, s.shape[1]), F32)
            for h in range(IDX_HEADS):
                tot = tot + w[:, h:h + 1] * jnp.maximum(s[h * T_PAD:(h + 1) * T_PAD], 0.0)
            return _order_key(tot)

        ikt = jnp.concatenate([r[0] for r in ik_refs], axis=1).astype(BF16)
        key_scr[:, pl.ds(pl.multiple_of(j * kt, kt), kt)] = score(ikt)

        @pl.when(j == last)
        def _():
            knew = score(kin_ref[0])
            key_scr[:, past:past + page] = jnp.where(new_visible(page), knew, INT_MIN)
            col = lax.broadcasted_iota(I32, (T_PAD, past + page), 1)

            def count_ge(cand):
                return jnp.sum(jnp.where(key_scr[...] >= cand, 1, 0).astype(I32), axis=-1, keepdims=True)

            def count_eq_below(thr, jj):
                hit = (key_scr[...] == thr) & (col < jj)
                return jnp.sum(jnp.where(hit, 1, 0).astype(I32), axis=-1, keepdims=True)

            thr, jmax = _select_threshold(count_ge, count_eq_below, (T_PAD, 1), past + page, nsel, nbits,
                                          tok < n_tok)
            thr_scr[...] = thr
            jmax_scr[...] = jmax
            m_scr[...] = jnp.full(m_scr.shape, NEG, F32)
            l_scr[...] = jnp.zeros(l_scr.shape, F32)
            acc_scr[...] = jnp.zeros(acc_scr.shape, F32)

    @pl.when(phase == 1)
    def _():
        q8 = qb_ref[0].astype(F32)
        qs = [jnp.concatenate([q8[:, h * HEAD_BLOCK:(h + 1) * HEAD_BLOCK] for h in range(p * hp, (p + 1) * hp)],
                              axis=0).astype(BF16) for p in range(2)]
        thr = thr_scr[...]
        jmax = jmax_scr[...]

        def attend(kt16, vt16, key, col0, extra_mask):
            width = key.shape[1]
            col = col0 + lax.broadcasted_iota(I32, (T_PAD, width), 1)
            sel = (key > thr) | ((key == thr) & (col <= jmax))
            if extra_mask is not None:
                sel = sel & extra_mask
            for p in range(2):
                s = _dot(qs[p], kt16[p * LANES:(p + 1) * LANES]).reshape(hp, T_PAD, width)
                s = jnp.where(sel[None], s, NEG)
                m = m_scr[p]
                m_new = jnp.maximum(m, jnp.max(s, axis=-1, keepdims=True))
                pe = jnp.exp2(s - m_new)
                corr = jnp.exp2(m - m_new)
                l_scr[p] = l_scr[p] * corr + jnp.sum(pe, axis=-1, keepdims=True)
                pv = _dot_nt(pe.reshape(hp * T_PAD, width).astype(BF16), vt16[p * LANES:(p + 1) * LANES])
                acc_scr[p] = acc_scr[p] * corr + pv.reshape(hp, T_PAD, LANES)
                m_scr[p] = m_new

        kt16 = jnp.concatenate([r[0] for r in k_refs], axis=1).astype(BF16)
        vt16 = jnp.concatenate([r[0] for r in v_refs], axis=1).astype(BF16)
        off = pl.multiple_of(j * kt, kt)
        attend(kt16, vt16, key_scr[:, pl.ds(off, kt)], off, None)

        @pl.when(j == last)
        def _():
            attend(kbn_ref[0], vbn_ref[0], key_scr[:, past:past + page], past, new_visible(page))
            o_ref[0] = _assemble_b_heads([acc_scr[p] / l_scr[p] for p in range(2)], T_PAD).astype(o_ref.dtype)


def _dsa_sample_call(page_table, qi, wi, qb, ki16, kb16, vb16, cache_ik_t, cache_k_t, cache_v_t, t):
    nseq, npages = page_table.shape
    page = cache_ik_t.shape[2]
    past = npages * page
    pps = min(PAGES_PER_STEP, npages)
    assert npages % pps == 0 and t <= T_PAD
    nj = npages // pps
    nsel = min(TOPK_MAX, (past + t) // 4)
    kvw = B_KV_HEADS * B_HEAD_DIM
    hp = B_HEADS // 2

    new_page = lambda a: _new_page_t(a, nseq, t, page)

    seq_spec = lambda r, w: pl.BlockSpec((1, r, w), lambda n, ph, j, pt: (n, 0, 0))
    ik_spec = lambda k: pl.BlockSpec(
        (1, IDX_DIM, page), lambda n, ph, j, pt, k=k: (pt[n, (j * (1 - ph) + (nj - 1) * ph) * pps + k], 0, 0))
    kv_spec = lambda k: pl.BlockSpec((1, kvw, page), lambda n, ph, j, pt, k=k: (pt[n, j * ph * pps + k], 0, 0))
    grid_spec = pltpu.PrefetchScalarGridSpec(
        num_scalar_prefetch=1,
        grid=(nseq, 2, nj),
        in_specs=[seq_spec(T_PAD, IDX_HEADS * HEAD_BLOCK), seq_spec(T_PAD, LANES), seq_spec(T_PAD, B_HEADS * HEAD_BLOCK),
                  seq_spec(IDX_DIM, page), seq_spec(kvw, page), seq_spec(kvw, page)]
                 + [ik_spec(k) for k in range(pps)] + [kv_spec(k) for k in range(pps)] * 2,
        out_specs=seq_spec(T_PAD, B_HEADS * B_HEAD_DIM),
        scratch_shapes=[pltpu.VMEM((T_PAD, past + page), I32), pltpu.VMEM((T_PAD, 1), I32), pltpu.VMEM((T_PAD, 1), I32),
                        pltpu.VMEM((2, hp, T_PAD, 1), F32), pltpu.VMEM((2, hp, T_PAD, 1), F32),
                        pltpu.VMEM((2, hp, T_PAD, LANES), F32)],
    )
    out = pl.pallas_call(
        functools.partial(_dsa_sample_kernel, pps=pps, page=page, nsel=nsel, nbits=int(past + page).bit_length(),
                          n_tok=t, past=past),
        grid_spec=grid_spec,
        out_shape=jax.ShapeDtypeStruct((nseq, T_PAD, B_HEADS * B_HEAD_DIM), BF16),
        compiler_params=pltpu.CompilerParams(dimension_semantics=("arbitrary",) * 3, vmem_limit_bytes=VMEM_LIMIT),
        name="dsa_sample",
    )(page_table, _pad_tokens(qi, nseq, t), _pad_tokens(wi, nseq, t), _pad_tokens(qb, nseq, t),
      new_page(ki16[:, :IDX_DIM]), new_page(kb16), new_page(vb16),
      *([cache_ik_t] * pps), *([cache_k_t] * pps), *([cache_v_t] * pps))
    return out[:, :t].reshape(nseq * t, B_HEADS * B_HEAD_DIM)


def _merge_kernel(x_ref, oa_ref, ob_ref, gattn_ref, wgate_ref, woa_ref, wob_ref, wout_ref, h_ref):
    x = x_ref[...]
    n = _rms_full(x, gattn_ref[...]).astype(BF16)
    gate = jax.nn.sigmoid(_dot(n, wgate_ref[...]))
    mix = gate[:, :D_MODEL] * _dot(oa_ref[...], woa_ref[...]) + gate[:, D_MODEL:] * _dot(ob_ref[...], wob_ref[...])
    h_ref[...] = x + _dot(mix.astype(BF16), wout_ref[...])


def _mlp_kernel(h_ref, gmlp_ref, wup_ref, wdown_ref, y_ref):
    h = h_ref[...]
    u = jnp.maximum(_dot(_rms_full(h, gmlp_ref[...]).astype(BF16), wup_ref[...]), 0.0)
    y_ref[...] = h + _dot((u * u).astype(BF16), wdown_ref[...])


def _output_call(x, o_a, o_b, g_attn, w_gate, p):
    m = x.shape[0]
    t = min(PROJ_TILE, m)
    row_spec = lambda w: pl.BlockSpec((t, w), lambda i: (i, 0))
    params = pltpu.CompilerParams(dimension_semantics=("arbitrary",), vmem_limit_bytes=VMEM_LIMIT)
    consts = [g_attn, w_gate, p['w_o_a'].astype(BF16), p['w_o_b'].astype(BF16), p['w_out'].astype(BF16)]
    h = pl.pallas_call(
        _merge_kernel,
        grid=(m // t,),
        in_specs=[row_spec(D_MODEL), row_spec(o_a.shape[1]), row_spec(o_b.shape[1])] + [_const_spec(a) for a in consts],
        out_specs=row_spec(D_MODEL),
        out_shape=jax.ShapeDtypeStruct((m, D_MODEL), F32),
        compiler_params=params,
        name="merge",
    )(x, o_a, o_b, *consts)
    consts = [p['g_mlp'].reshape(1, -1), p['w_up'].astype(BF16), p['w_down'].astype(BF16)]
    return pl.pallas_call(
        _mlp_kernel,
        grid=(m // t,),
        in_specs=[row_spec(D_MODEL)] + [_const_spec(a) for a in consts],
        out_specs=row_spec(D_MODEL),
        out_shape=jax.ShapeDtypeStruct((m, D_MODEL), F32),
        compiler_params=params,
        name="mlp",
    )(h, *consts)


def _layer(h_p, h_s, caches, page_table, p):
    n, s, _ = h_p.shape
    nseq, t, _ = h_s.shape
    past = page_table.shape[1] * caches['lat'].shape[1]
    proj_params, extra = _prep_proj_params(p)
    g_attn = proj_params[0]

    xp = h_p.reshape(n * s, D_MODEL)
    (qa, kcat, va, lat, krope, qb, kb32, kb16, vb32, vb16, qi, ki32, ki16, wi) = _proj_call(
        xp, _rope_tables(jnp.arange(s)), proj_params)
    o_a = _mla_prompt_call(qa, kcat, va, n, s)
    o_b = _dsa_prompt_call(qi, wi, qb, ki16, kb16, vb16, n, s)
    y_p = _output_call(xp, o_a, o_b, g_attn, extra['w_gate'], p).reshape(n, s, D_MODEL)
    new_p = (lat.reshape(n, s, -1), krope.reshape(n, s, -1), kb32.reshape(n, s, B_KV_HEADS, B_HEAD_DIM),
             vb32.reshape(n, s, B_KV_HEADS, B_HEAD_DIM), ki32.reshape(n, s, -1))

    xs = h_s.reshape(nseq * t, D_MODEL)
    pos_s = jnp.tile(past + jnp.arange(t), nseq)
    (qa, kcat, va, lat, krope, qb, kb32, kb16, vb32, vb16, qi, ki32, ki16, wi) = _proj_call(
        xs, _rope_tables(pos_s), proj_params)
    o_a = _mla_sample_call(page_table, qa, lat, krope, caches['lat'], _pages_t(caches['kr']), extra, t)
    o_b = _dsa_sample_call(page_table, qi, wi, qb, ki16, kb16, vb16, _pages_t(caches['ik']),
                           _pages_t(caches['k']), _pages_t(caches['v']), t)
    y_s = _output_call(xs, o_a, o_b, g_attn, extra['w_gate'], p).reshape(nseq, t, D_MODEL)
    new_s = (lat.reshape(nseq, t, -1), krope.reshape(nseq, t, -1), kb32.reshape(nseq, t, B_KV_HEADS, B_HEAD_DIM),
             vb32.reshape(nseq, t, B_KV_HEADS, B_HEAD_DIM), ki32.reshape(nseq, t, -1))
    return y_p, y_s, new_p, new_s


def kernel(x_prompt, x_sample, cache_mla_latent, cache_mla_krope, cache_dsa_k, cache_dsa_v, cache_idx_k, page_table, g_attn, w_in, g_q_a, w_uq, g_qn_a, g_qr_a, g_kv_a, g_kr_a, w_uk, g_kn_a, w_uv, w_o_a, g_q_b, g_k_b, w_o_b, w_out, g_mlp, w_up, w_down):
    depth = w_in.shape[0]
    h_p, h_s = x_prompt, x_sample
    new_p, new_s = [], []
    for l in range(depth):
        p = dict(g_attn=g_attn[l], w_in=w_in[l], g_q_a=g_q_a[l], w_uq=w_uq[l], g_qn_a=g_qn_a[l],
                 g_qr_a=g_qr_a[l], g_kv_a=g_kv_a[l], g_kr_a=g_kr_a[l], w_uk=w_uk[l], g_kn_a=g_kn_a[l],
                 w_uv=w_uv[l], w_o_a=w_o_a[l], g_q_b=g_q_b[l], g_k_b=g_k_b[l], w_o_b=w_o_b[l],
                 w_out=w_out[l], g_mlp=g_mlp[l], w_up=w_up[l], w_down=w_down[l])
        caches = dict(lat=cache_mla_latent[l], kr=cache_mla_krope[l], k=cache_dsa_k[l], v=cache_dsa_v[l],
                      ik=cache_idx_k[l])
        h_p, h_s, np_l, ns_l = _layer(h_p, h_s, caches, page_table, p)
        new_p.append(np_l)
        new_s.append(ns_l)
    stack = lambda parts, i: jnp.stack([q[i] for q in parts])
    return (h_p, h_s, *[stack(new_p, i) for i in range(5)], *[stack(new_s, i) for i in range(5)])
```

```python
import functools
import math

import jax
import jax.numpy as jnp
import numpy as np
from jax import lax
from jax.experimental import pallas as pl
from jax.experimental.pallas import tpu as pltpu

F32 = jnp.float32
BF16 = jnp.bfloat16
I32 = jnp.int32

D_MODEL = 1024
A_HEADS = 8
A_NOPE = 64
A_ROPE = 32
A_V = 64
A_Q_RANK = 768
A_KV_RANK = 256
A_SCALE = 1.0 / math.sqrt(A_NOPE + A_ROPE)
B_HEADS = 8
B_KV_HEADS = 4
B_HEAD_DIM = 64
B_SCALE = 1.0 / math.sqrt(B_HEAD_DIM)
IDX_HEADS = 8
IDX_DIM = 64
IDX_SCALE = 1.0 / math.sqrt(IDX_DIM)
TOPK_MAX = 256
D_FF = 4 * D_MODEL
ROPE_THETA = 10000.0
EPS = 1e-6
IN_SIZES = (A_Q_RANK, A_KV_RANK, A_ROPE, B_HEADS * B_HEAD_DIM, B_KV_HEADS * B_HEAD_DIM,
            B_KV_HEADS * B_HEAD_DIM, IDX_HEADS * IDX_DIM, IDX_DIM, IDX_HEADS, 2 * D_MODEL)

LANES = 128
HEAD_BLOCK = LANES
NEG = -1e30
INT_MIN = -(2 ** 31)
VMEM_LIMIT = 56 * 1024 * 1024

LOG2E = math.log2(math.e)

PROJ_TILE = 256
MLA_TQ = 512
DSA_TQ = 256
DSA_TK = 512
COUNT_ROWS = 64
PAGES_PER_STEP = 16
PAGES_PER_CHUNK = 8
T_PAD = 8


def _dot(a, b):
    return jnp.dot(a, b, preferred_element_type=F32)


def _dot_nt(a, b):
    return lax.dot_general(a, b, (((1,), (1,)), ((), ())), preferred_element_type=F32)


def _rms_full(x, g):
    ms = jnp.mean(x * x, axis=-1, keepdims=True)
    return x * lax.rsqrt(ms + EPS) * g


def _split_bf16(x):
    hi = x.astype(BF16)
    lo = (x - hi.astype(F32)).astype(BF16)
    return hi, lo


def _group_rms(x, gs_ref, inv_ref, e_ref):
    hi, lo = _split_bf16(x * x)
    gs = gs_ref[...]
    ssum = _dot(hi, gs) + _dot(lo, gs)
    r = lax.rsqrt(ssum * inv_ref[...] + EPS)
    rhi, rlo = _split_bf16(r)
    e = e_ref[...]
    return x * (_dot(rhi, e) + _dot(rlo, e))


def _rope(x, tabs, half):
    c, sa, sb = tabs
    outs = []
    for j in range(x.shape[1] // LANES):
        xc = x[:, j * LANES:(j + 1) * LANES]
        outs.append(xc * c + pltpu.roll(xc, LANES - half, 1) * sa + pltpu.roll(xc, half, 1) * sb)
    return outs[0] if len(outs) == 1 else jnp.concatenate(outs, axis=1)


def _proj_kernel(x_ref, gattn_ref, ca_ref, saa_ref, sba_ref, cb_ref, sab_ref, sbb_ref,
                 wqa_ref, gqa_ref, wuq_ref, gsqa_ref, invqa_ref, eqa_ref, gainqa_ref,
                 wckv_ref, gkva_ref, wkr_ref, gainkr_ref,
                 wuk_ref, gskn_ref, invkn_ref, ekn_ref, gainkn_ref, wuv_ref,
                 wqb_ref, gsqb_ref, invqb_ref, eqb_ref, gainqb_ref,
                 wkb_ref, gskb_ref, invkb_ref, ekb_ref, gainkb_ref,
                 wvb_ref, wqi_ref, wki_ref, wwi_ref,
                 qa_o, kcat_o, va_o, lat_o, krope_o, qb_o, kb32_o, kb16_o, vb32_o, vb16_o,
                 qi_o, ki32_o, ki16_o, wi_o):
    n = _rms_full(x_ref[...], gattn_ref[...]).astype(BF16)
    tab_a = (ca_ref[...], saa_ref[...], sba_ref[...])
    tab_b = (cb_ref[...], sab_ref[...], sbb_ref[...])

    qlat = _rms_full(_dot(n, wqa_ref[...]), gqa_ref[...]).astype(BF16)
    qa = _group_rms(_dot(qlat, wuq_ref[...]), gsqa_ref, invqa_ref, eqa_ref) * gainqa_ref[...]
    qa_o[...] = _rope(qa, tab_a, A_ROPE // 2).astype(BF16)

    c = _rms_full(_dot(n, wckv_ref[...]), gkva_ref[...])
    lat_o[...] = c
    c16 = c.astype(BF16)
    kr = _dot(n, wkr_ref[...])
    ms = jnp.sum(kr * kr, axis=-1, keepdims=True) * (1.0 / A_ROPE)
    kr = _rope(kr * lax.rsqrt(ms + EPS) * gainkr_ref[...], tab_a, A_ROPE // 2)
    krope_o[...] = kr[:, A_NOPE:A_NOPE + A_ROPE]
    kn = _group_rms(_dot(c16, wuk_ref[...]), gskn_ref, invkn_ref, ekn_ref) * gainkn_ref[...]
    kcat_o[...] = (kn + jnp.concatenate([kr] * A_HEADS, axis=1)).astype(BF16)
    va_o[...] = _dot(c16, wuv_ref[...]).astype(BF16)

    qb = _group_rms(_dot(n, wqb_ref[...]), gsqb_ref, invqb_ref, eqb_ref) * gainqb_ref[...]
    qb_o[...] = _rope(qb, tab_b, B_HEAD_DIM // 2).astype(BF16)
    kb = _group_rms(_dot(n, wkb_ref[...]), gskb_ref, invkb_ref, ekb_ref) * gainkb_ref[...]
    kb = _rope(kb, tab_b, B_HEAD_DIM // 2)
    kb32_o[...] = kb
    kb16_o[...] = kb.astype(BF16)
    vb = _dot(n, wvb_ref[...])
    vb32_o[...] = vb
    vb16_o[...] = vb.astype(BF16)

    qi = _rope(_dot(n, wqi_ref[...]), tab_b, IDX_DIM // 2) * IDX_SCALE
    qi_o[...] = qi.astype(BF16)
    ki = _rope(_dot(n, wki_ref[...]), tab_b, IDX_DIM // 2)
    ki32_o[...] = ki[:, :IDX_DIM]
    ki16_o[...] = ki.astype(BF16)
    wi_o[...] = _dot(n, wwi_ref[...]) * (IDX_HEADS ** -0.5)


def _const_spec(a):
    nd = a.ndim
    return pl.BlockSpec(a.shape, lambda *_: (0,) * nd)


def _group_mats(width, groups):
    gs = np.zeros((width, LANES), np.float32)
    inv = np.ones((1, LANES), np.float32)
    for j, (start, size) in enumerate(groups):
        gs[start:start + size, j] = 1.0
        inv[0, j] = 1.0 / size
    return jnp.asarray(gs, BF16), jnp.asarray(inv), jnp.asarray(gs.T, BF16)


def _b_half(h):
    return (h // (B_HEADS // B_KV_HEADS)) % 2


def _rope_tables(pos):
    posf = pos.astype(F32)[:, None]

    def cs(d):
        inv = ROPE_THETA ** (-jnp.arange(0, d, 2, dtype=F32) / d)
        ang = posf * inv[None, :]
        return jnp.cos(ang), jnp.sin(ang)

    r = pos.shape[0]
    c16, s16 = cs(A_ROPE)
    one, zero = jnp.ones((r, 1), F32), jnp.zeros((r, 1), F32)
    z16 = jnp.zeros((r, A_ROPE // 2), F32)
    ca = jnp.concatenate([jnp.tile(one, (1, A_NOPE)), c16, c16, jnp.tile(one, (1, 32))], axis=1)
    saa = jnp.concatenate([jnp.tile(zero, (1, A_NOPE)), -s16, z16, jnp.tile(zero, (1, 32))], axis=1)
    sba = jnp.concatenate([jnp.tile(zero, (1, A_NOPE)), z16, s16, jnp.tile(zero, (1, 32))], axis=1)
    c32, s32 = cs(B_HEAD_DIM)
    z32 = jnp.zeros_like(s32)
    cb = jnp.concatenate([c32] * 4, axis=1)
    sab = jnp.concatenate([-s32, z32] * 2, axis=1)
    sbb = jnp.concatenate([z32, s32] * 2, axis=1)
    return ca, saa, sba, cb, sab, sbb


def _prep_proj_params(p):
    offs = np.cumsum((0,) + IN_SIZES)
    w_in = p['w_in']
    w = [w_in[:, offs[i]:offs[i + 1]] for i in range(len(IN_SIZES))]
    w_qa, w_ckv, w_kr, w_qb, w_kb, w_vb, w_qi, w_ki, w_wi, w_gate = w
    d = D_MODEL

    def blocks(wm, heads, dim):
        wm = wm.reshape(wm.shape[0], heads, dim)
        return jnp.pad(wm, ((0, 0), (0, 0), (0, HEAD_BLOCK - dim))).reshape(wm.shape[0], heads * HEAD_BLOCK)

    w_uq = blocks(p['w_uq'], A_HEADS, A_NOPE + A_ROPE)
    w_uk_c = p['w_uk'].reshape(A_KV_RANK, A_HEADS * A_NOPE)
    w_uk = blocks(w_uk_c, A_HEADS, A_NOPE)
    w_uv = p['w_uv'].reshape(A_KV_RANK, A_HEADS * A_V)
    w_kr = jnp.pad(w_kr, ((0, 0), (A_NOPE, LANES - A_NOPE - A_ROPE)))
    qb_cols, gqb = [], []
    zc = jnp.zeros((d, B_HEAD_DIM), F32)
    zg = jnp.zeros((B_HEAD_DIM,), F32)
    for h in range(B_HEADS):
        wh = w_qb[:, h * B_HEAD_DIM:(h + 1) * B_HEAD_DIM]
        qb_cols += [zc, wh] if _b_half(h) else [wh, zc]
        gqb += [zg, p['g_q_b']] if _b_half(h) else [p['g_q_b'], zg]
    w_qb = jnp.concatenate(qb_cols, axis=1)
    w_qi = blocks(w_qi, IDX_HEADS, IDX_DIM)
    w_ki = jnp.pad(w_ki, ((0, 0), (0, LANES - IDX_DIM)))
    w_wi = jnp.pad(w_wi, ((0, 0), (0, LANES - IDX_HEADS)))

    z32 = jnp.zeros((HEAD_BLOCK - A_NOPE - A_ROPE,), F32)
    gain_qa = jnp.tile(jnp.concatenate([p['g_qn_a'], p['g_qr_a'], z32]), A_HEADS) * (A_SCALE * LOG2E)
    gain_kn = jnp.tile(jnp.concatenate([p['g_kn_a'], jnp.zeros((HEAD_BLOCK - A_NOPE,), F32)]), A_HEADS)
    gain_kr = jnp.concatenate([jnp.zeros((A_NOPE,), F32), p['g_kr_a'], z32])
    gain_qb = jnp.concatenate(gqb) * (B_SCALE * LOG2E)
    gain_kb = jnp.tile(p['g_k_b'], B_KV_HEADS)

    qa_groups = []
    for h in range(A_HEADS):
        qa_groups += [(h * HEAD_BLOCK, A_NOPE), (h * HEAD_BLOCK + A_NOPE, A_ROPE)]
    kn_groups = [(h * HEAD_BLOCK, A_NOPE) for h in range(A_HEADS)]
    qb_groups = [(h * HEAD_BLOCK + B_HEAD_DIM * _b_half(h), B_HEAD_DIM) for h in range(B_HEADS)]
    kb_groups = [(g * B_HEAD_DIM, B_HEAD_DIM) for g in range(B_KV_HEADS)]

    row = lambda v: v.reshape(1, -1).astype(F32)
    b16 = lambda m: m.astype(BF16)
    proj = [row(p['g_attn']),
            b16(w_qa), row(p['g_q_a']), b16(w_uq), *_group_mats(A_HEADS * HEAD_BLOCK, qa_groups), row(gain_qa),
            b16(w_ckv), row(p['g_kv_a']), b16(w_kr), row(gain_kr),
            b16(w_uk), *_group_mats(A_HEADS * HEAD_BLOCK, kn_groups), row(gain_kn), b16(w_uv),
            b16(w_qb), *_group_mats(B_HEADS * HEAD_BLOCK, qb_groups), row(gain_qb),
            b16(w_kb), *_group_mats(B_KV_HEADS * B_HEAD_DIM, kb_groups), row(gain_kb),
            b16(w_vb), b16(w_qi), b16(w_ki), b16(w_wi)]
    extra = dict(w_gate=b16(w_gate), w_uk_blocks=b16(w_uk), w_uk_compact=b16(w_uk_c), w_uv=b16(w_uv),
                 g_kn=row(p['g_kn_a']))
    return proj, extra


def _proj_call(x, tables, proj_params):
    m = x.shape[0]
    t = min(PROJ_TILE, m)
    assert m % t == 0 and tables[0].shape[0] % t == 0
    nt = tables[0].shape[0] // t
    row_spec = lambda w: pl.BlockSpec((t, w), lambda i: (i, 0))
    tab_spec = pl.BlockSpec((t, LANES), lambda i: (i % nt, 0))
    hb = A_HEADS * HEAD_BLOCK
    outs = [(hb, BF16), (hb, BF16), (A_HEADS * A_V, BF16), (A_KV_RANK, F32), (A_ROPE, F32),
            (B_HEADS * HEAD_BLOCK, BF16), (B_KV_HEADS * B_HEAD_DIM, F32), (B_KV_HEADS * B_HEAD_DIM, BF16),
            (B_KV_HEADS * B_HEAD_DIM, F32), (B_KV_HEADS * B_HEAD_DIM, BF16),
            (IDX_HEADS * HEAD_BLOCK, BF16), (IDX_DIM, F32), (LANES, BF16), (LANES, F32)]
    return pl.pallas_call(
        _proj_kernel,
        grid=(m // t,),
        in_specs=[row_spec(D_MODEL), _const_spec(proj_params[0])] + [tab_spec] * 6
                 + [_const_spec(a) for a in proj_params[1:]],
        out_specs=[row_spec(w) for w, _ in outs],
        out_shape=[jax.ShapeDtypeStruct((m, w), dt) for w, dt in outs],
        compiler_params=pltpu.CompilerParams(dimension_semantics=("arbitrary",), vmem_limit_bytes=VMEM_LIMIT),
        name="proj",
    )(x, proj_params[0], *tables, *proj_params[1:])


def _mla_prompt_kernel(q_ref, k_ref, v_ref, o_ref, *, tq):
    i = pl.program_id(2)
    rows = lax.broadcasted_iota(I32, (tq, tq), 0)
    cols = lax.broadcasted_iota(I32, (tq, tq), 1)
    causal = cols <= rows
    lane = lax.broadcasted_iota(I32, (tq, LANES), 1)
    qs = [q_ref[:, hh * HEAD_BLOCK:(hh + 1) * HEAD_BLOCK] for hh in range(2)]

    def step(j, carry, masked):
        off = pl.multiple_of(j * tq, tq)
        v = v_ref[pl.ds(off, tq), :]
        new = []
        for hh in range(2):
            m, l, acc = carry[hh]
            s = _dot_nt(qs[hh], k_ref[pl.ds(off, tq), hh * HEAD_BLOCK:(hh + 1) * HEAD_BLOCK])
            if masked:
                s = jnp.where(causal, s, NEG)
            m_new = jnp.maximum(m, jnp.max(s, axis=-1, keepdims=True))
            p = jnp.exp2(s - m_new)
            corr = jnp.exp2(m - m_new)
            l = l * corr + jnp.sum(p, axis=-1, keepdims=True)
            new.append((m_new, l, acc * corr + _dot(p.astype(BF16), v)))
        return tuple(new)

    init = tuple((jnp.full((tq, 1), NEG, F32), jnp.zeros((tq, 1), F32), jnp.zeros((tq, LANES), F32))
                 for _ in range(2))
    carry = lax.fori_loop(0, i, functools.partial(step, masked=False), init)
    res = step(i, carry, masked=True)
    outs = [res[hh][2] / res[hh][1] for hh in range(2)]
    o_ref[...] = jnp.where(lane < A_V, outs[0], outs[1]).astype(o_ref.dtype)


def _mla_prompt_call(qa, kcat, va, n, s):
    tq = min(MLA_TQ, s)
    nq = s // tq
    pairs = A_HEADS // 2
    return pl.pallas_call(
        functools.partial(_mla_prompt_kernel, tq=tq),
        grid=(n, pairs, nq),
        in_specs=[pl.BlockSpec((tq, 2 * HEAD_BLOCK), lambda b, h, i: (b * nq + i, h)),
                  pl.BlockSpec((s, 2 * HEAD_BLOCK), lambda b, h, i: (b, h)),
                  pl.BlockSpec((s, 2 * A_V), lambda b, h, i: (b, h))],
        out_specs=pl.BlockSpec((tq, 2 * A_V), lambda b, h, i: (b * nq + i, h)),
        out_shape=jax.ShapeDtypeStruct((n * s, A_HEADS * A_V), BF16),
        compiler_params=pltpu.CompilerParams(dimension_semantics=("arbitrary",) * 3, vmem_limit_bytes=VMEM_LIMIT),
        name="mla_prompt",
    )(qa, kcat, va)


def _order_key(score):
    bits = pltpu.bitcast(score, I32)
    return bits ^ ((bits >> 31) & 0x7FFFFFFF)


def _select_threshold(count_ge, count_eq_below, shape, n_all, nsel, nbits, row_ok):
    c0 = count_ge(jnp.zeros(shape, I32))
    thr = jnp.where(c0 >= nsel, 0, INT_MIN).astype(I32)
    n_ge = jnp.where(c0 >= nsel, c0, n_all).astype(I32)

    def bit_body(p, carry):
        thr, n_ge = carry
        cand = thr + lax.shift_left(jnp.int32(1), 30 - p)
        cnt = count_ge(cand)
        ok = cnt >= nsel
        return jnp.where(ok, cand, thr), jnp.where(ok, cnt, n_ge)

    thr, n_ge = lax.fori_loop(0, 31, bit_body, (thr, n_ge))
    tie = (n_ge > nsel) & (thr > INT_MIN) & row_ok
    all_cols = jnp.full(shape, 2 ** 30, I32)

    def resolve():
        need = nsel - jnp.where(thr == 2 ** 31 - 1, 0, count_ge(thr + 1))

        def jbit(p, jmax):
            cand = jmax | lax.shift_left(jnp.int32(1), nbits - 1 - p)
            return jnp.where(count_eq_below(thr, cand) < need, cand, jmax)
        jmax = lax.fori_loop(0, nbits, jbit, jnp.zeros(shape, I32))
        return jnp.where(tie, jmax, all_cols)

    jmax = lax.cond(jnp.max(tie.astype(I32)) > 0, resolve, lambda: all_cols)
    return thr, jmax


def _dsa_prompt_kernel(qi_ref, wi_ref, qb_ref, ki_ref, kb_ref, vb_ref, o_ref, key_scr, *, tq, tk, nsel, nbits):
    i = pl.program_id(1)
    q0 = i * tq
    nch = q0 // tk + 1
    row_g = q0 + lax.broadcasted_iota(I32, (tq, tk), 0)
    col_l = lax.broadcasted_iota(I32, (tq, tk), 1)

    qi = jnp.concatenate([qi_ref[:, h * HEAD_BLOCK:(h + 1) * HEAD_BLOCK] for h in range(IDX_HEADS)], axis=0)
    w = wi_ref[...]
    wcol = [w[:, h:h + 1] for h in range(IDX_HEADS)]

    def idx_chunk(c, carry):
        off = pl.multiple_of(c * tk, tk)
        s = _dot_nt(qi, ki_ref[pl.ds(off, tk), :])
        score = jnp.zeros((tq, tk), F32)
        for h in range(IDX_HEADS):
            score = score + wcol[h] * jnp.maximum(s[h * tq:(h + 1) * tq], 0.0)
        key_scr[:, pl.ds(off, tk)] = jnp.where(off + col_l <= row_g, _order_key(score), INT_MIN)
        return carry

    lax.fori_loop(0, nch, idx_chunk, 0)

    rb = min(COUNT_ROWS, tq)
    lane_l = lax.broadcasted_iota(I32, (rb, LANES), 1)

    def count(pred):
        def body(c, accs):
            off = pl.multiple_of(c * tk, tk)
            new = []
            for b in range(tq // rb):
                sl = slice(b * rb, (b + 1) * rb)
                acc = accs[b]
                for jj in range(tk // LANES):
                    k = key_scr[sl, pl.ds(off + jj * LANES, LANES)]
                    acc = acc + jnp.where(pred(k, off + jj * LANES + lane_l, sl), 1, 0).astype(I32)
                new.append(acc)
            return tuple(new)
        accs = lax.fori_loop(0, nch, body, tuple(jnp.zeros((rb, LANES), I32) for _ in range(tq // rb)))
        tot = jnp.sum(jnp.concatenate(accs, axis=0), axis=-1, keepdims=True)
        return jnp.broadcast_to(tot, (tq, LANES))

    count_ge = lambda cand: count(lambda k, col, sl: k >= cand[sl])
    count_eq_below = lambda thr, j: count(lambda k, col, sl: (k == thr[sl]) & (col < j[sl]))
    thr, jmax = _select_threshold(count_ge, count_eq_below, (tq, LANES), nch * tk, nsel, nbits,
                                  lax.broadcasted_iota(I32, (tq, LANES), 0) >= 0)
    thr, jmax = thr[:, :1], jmax[:, :1]

    hp = B_HEADS // 2
    qs = [jnp.concatenate([qb_ref[:, h * HEAD_BLOCK:(h + 1) * HEAD_BLOCK] for h in range(p * hp, (p + 1) * hp)],
                          axis=0) for p in range(2)]

    def att_chunk(c, carry):
        off = pl.multiple_of(c * tk, tk)
        key = key_scr[:, pl.ds(off, tk)]
        col = off + col_l
        sel = ((key > thr) | ((key == thr) & (col <= jmax))) & (col <= row_g)
        new = []
        for p in range(2):
            m, l, acc = carry[p]
            kc = kb_ref[pl.ds(off, tk), p * LANES:(p + 1) * LANES]
            vc = vb_ref[pl.ds(off, tk), p * LANES:(p + 1) * LANES]
            s = jnp.where(sel[None], _dot_nt(qs[p], kc).reshape(hp, tq, tk), NEG)
            m_new = jnp.maximum(m, jnp.max(s, axis=-1, keepdims=True))
            pe = jnp.exp2(s - m_new)
            corr = jnp.exp2(m - m_new)
            l = l * corr + jnp.sum(pe, axis=-1, keepdims=True)
            pv = _dot(pe.reshape(hp * tq, tk).astype(BF16), vc).reshape(hp, tq, LANES)
            new.append((m_new, l, acc * corr + pv))
        return tuple(new)

    init = tuple((jnp.full((hp, tq, 1), NEG, F32), jnp.zeros((hp, tq, 1), F32), jnp.zeros((hp, tq, LANES), F32))
                 for _ in range(2))
    res = lax.fori_loop(0, nch, att_chunk, init)
    o_ref[...] = _assemble_b_heads([res[p][2] / res[p][1] for p in range(2)], tq).astype(o_ref.dtype)


def _assemble_b_heads(accs, rows):
    lane = lax.broadcasted_iota(I32, (rows, LANES), 1)
    swap = lambda a: pltpu.roll(a, B_HEAD_DIM, 1)
    blocks = []
    for p in range(2):
        a = accs[p]
        blocks.append(jnp.where(lane < B_HEAD_DIM, a[0], swap(a[1])))
        blocks.append(jnp.where(lane < B_HEAD_DIM, swap(a[2]), a[3]))
    return jnp.concatenate(blocks, axis=1)


def _dsa_prompt_call(qi, wi, qb, ki16, kb16, vb16, n, s):
    tq = min(DSA_TQ, s)
    tk = min(DSA_TK, s)
    nq = s // tq
    nsel = min(TOPK_MAX, s // 4)
    kvw = B_KV_HEADS * B_HEAD_DIM
    qspec = lambda w: pl.BlockSpec((tq, w), lambda b, i: (b * nq + i, 0))
    kspec = lambda w: pl.BlockSpec((s, w), lambda b, i: (b, 0))
    return pl.pallas_call(
        functools.partial(_dsa_prompt_kernel, tq=tq, tk=tk, nsel=nsel, nbits=int(s).bit_length()),
        grid=(n, nq),
        in_specs=[qspec(IDX_HEADS * HEAD_BLOCK), qspec(LANES), qspec(B_HEADS * HEAD_BLOCK),
                  kspec(LANES), kspec(kvw), kspec(kvw)],
        out_specs=qspec(B_HEADS * B_HEAD_DIM),
        out_shape=jax.ShapeDtypeStruct((n * s, B_HEADS * B_HEAD_DIM), BF16),
        scratch_shapes=[pltpu.VMEM((tq, s), I32)],
        compiler_params=pltpu.CompilerParams(dimension_semantics=("arbitrary",) * 2, vmem_limit_bytes=VMEM_LIMIT),
        name="dsa_prompt",
    )(qi, wi, qb, ki16, kb16, vb16)


def _mla_sample_kernel(pt_ref, q_ref, latn_ref, krn_ref, qmask_ref, wukb_ref, wukt_ref, wuv_ref,
                       hmask_ref, place_ref, lat_hbm, kr_hbm, o_ref, lhs_scr, qr_scr, m_scr, l_scr, acc_scr,
                       lat_buf, kr_buf, sem, *, pps, ppc, page, n_tok, nseq, nj):
    n = pl.program_id(0)
    j = pl.program_id(1)
    rows = A_HEADS * T_PAD
    knw = A_HEADS * A_NOPE

    def page_copies(seq, jj, slot):
        cps = []
        for k in range(pps):
            pid = pt_ref[seq, jj * pps + k]
            cps.append(pltpu.make_async_copy(lat_hbm.at[pid], lat_buf.at[slot, k], sem.at[0, slot]))
            cps.append(pltpu.make_async_copy(kr_hbm.at[pid], kr_buf.at[slot, k], sem.at[1, slot]))
        return cps

    step = n * nj + j
    slot = step & 1

    @pl.when(step == 0)
    def _():
        for cp in page_copies(n, j, slot):
            cp.start()

    @pl.when(step + 1 < nseq * nj)
    def _():
        wrap = j + 1 == nj
        for cp in page_copies(jnp.where(wrap, n + 1, n), jnp.where(wrap, 0, j + 1), 1 - slot):
            cp.start()

    for cp in page_copies(n, j, slot):
        cp.wait()

    @pl.when(j == 0)
    def _():
        q8 = q_ref[0].astype(F32)
        qpad = (jnp.concatenate([q8] * A_HEADS, axis=0) * qmask_ref[...]).astype(BF16)
        lhs_scr[:knw, :] = wukt_ref[...]
        lhs_scr[knw:, :] = _dot_nt(qpad, wukb_ref[...]).astype(BF16)
        qr = jnp.concatenate([q8[:, h * HEAD_BLOCK:(h + 1) * HEAD_BLOCK] for h in range(A_HEADS)], axis=0)
        qr_scr[...] = _dot(qr.astype(BF16), place_ref[...]).astype(BF16)
        m_scr[...] = jnp.full((rows, 1), NEG, F32)
        l_scr[...] = jnp.zeros((rows, 1), F32)
        acc_scr[...] = jnp.zeros((rows, A_KV_RANK), F32)

    def partial(c, krt, mask):
        nk = c.shape[0]
        c16 = c.astype(BF16)
        both = _dot_nt(lhs_scr[...], c16)
        knt = both[:knw]
        msq = jnp.sum((knt * knt).reshape(A_HEADS, A_NOPE, nk), axis=1) * (1.0 / A_NOPE)
        r = lax.rsqrt(msq + EPS)
        r = jnp.broadcast_to(r[:, None, :], (A_HEADS, T_PAD, nk)).reshape(rows, nk)
        s = both[knw:] * r + _dot(qr_scr[...], krt.astype(BF16))
        if mask is not None:
            s = jnp.where(mask, s, NEG)
        mg = jnp.max(s, axis=-1, keepdims=True)
        p = jnp.exp2(s - mg)
        return mg, jnp.sum(p, axis=-1, keepdims=True), _dot(p.astype(BF16), c16)

    def merge(parts):
        m = m_scr[...]
        m_new = m
        for mg, _, _ in parts:
            m_new = jnp.maximum(m_new, mg)
        corr = jnp.exp2(m - m_new)
        l = l_scr[...] * corr
        acc = acc_scr[...] * corr
        for mg, lg, pv in parts:
            w = jnp.exp2(mg - m_new)
            l = l + lg * w
            acc = acc + pv * w
        m_scr[...] = m_new
        l_scr[...] = l
        acc_scr[...] = acc

    parts = []
    for g in range(pps // ppc):
        c = jnp.concatenate([lat_buf[slot, g * ppc + k] for k in range(ppc)], axis=0)
        krt = jnp.concatenate([kr_buf[slot, g * ppc + k] for k in range(ppc)], axis=1)
        parts.append(partial(c, krt, None))
    merge(parts)

    @pl.when(j == pl.num_programs(1) - 1)
    def _():
        r = lax.broadcasted_iota(I32, (rows, page), 0) & (T_PAD - 1)
        col = lax.broadcasted_iota(I32, (rows, page), 1)
        merge([partial(latn_ref[0], krn_ref[0], (col <= r) & (col < n_tok))])
        lat_out = (acc_scr[...] / l_scr[...]).astype(BF16)
        full = _dot(lat_out, wuv_ref[...])
        hm = hmask_ref[...]
        out = jnp.zeros((T_PAD, A_HEADS * A_V), F32)
        for h in range(A_HEADS):
            out = out + full[h * T_PAD:(h + 1) * T_PAD] * hm[h:h + 1]
        o_ref[0] = out.astype(o_ref.dtype)


def _pad_tokens(a, nseq, t, rows=T_PAD):
    a = a.reshape(nseq, t, a.shape[-1])
    return jnp.pad(a, ((0, 0), (0, rows - t), (0, 0)))


def _new_page_t(a, nseq, t, page):
    return jnp.swapaxes(_pad_tokens(a, nseq, t, page), 1, 2)


def _pages_t(cache):
    pool, page = cache.shape[:2]
    return jnp.moveaxis(cache, 1, -1).reshape(pool, -1, page)


def _mla_sample_call(page_table, qa, lat_new, kr_new, cache_lat, cache_kr_t, extra, t):
    nseq, npages = page_table.shape
    page = cache_lat.shape[1]
    pps = min(PAGES_PER_STEP, npages)
    ppc = min(PAGES_PER_CHUNK, pps)
    assert npages % pps == 0 and pps % ppc == 0
    rows = A_HEADS * T_PAD
    hb = A_HEADS * HEAD_BLOCK

    qmask = np.zeros((rows, hb), np.float32)
    hmask = np.zeros((A_HEADS, A_HEADS * A_V), np.float32)
    for h in range(A_HEADS):
        qmask[h * T_PAD:(h + 1) * T_PAD, h * HEAD_BLOCK:h * HEAD_BLOCK + A_NOPE] = 1.0
        hmask[h, h * A_V:(h + 1) * A_V] = 1.0
    gkn_blocks = jnp.tile(jnp.pad(extra['g_kn'], ((0, 0), (0, HEAD_BLOCK - A_NOPE))), (1, A_HEADS))
    qmask = jnp.asarray(qmask) * gkn_blocks
    place = np.zeros((HEAD_BLOCK, A_ROPE), np.float32)
    place[A_NOPE:A_NOPE + A_ROPE, :] = np.eye(A_ROPE, dtype=np.float32)
    consts = [qmask, extra['w_uk_blocks'], extra['w_uk_compact'].T, extra['w_uv'],
              jnp.asarray(hmask), jnp.asarray(place, BF16)]

    seq_spec = lambda r, w: pl.BlockSpec((1, r, w), lambda n, j, pt: (n, 0, 0))
    cspec = lambda a: pl.BlockSpec(a.shape, lambda n, j, pt: (0,) * a.ndim)
    hbm_spec = pl.BlockSpec(memory_space=pl.ANY)
    nj = npages // pps
    grid_spec = pltpu.PrefetchScalarGridSpec(
        num_scalar_prefetch=1,
        grid=(nseq, nj),
        in_specs=[seq_spec(T_PAD, hb), seq_spec(page, A_KV_RANK), seq_spec(A_ROPE, page)] + [cspec(a) for a in consts]
                 + [hbm_spec, hbm_spec],
        out_specs=seq_spec(T_PAD, A_HEADS * A_V),
        scratch_shapes=[pltpu.VMEM((A_HEADS * A_NOPE + rows, A_KV_RANK), BF16), pltpu.VMEM((rows, A_ROPE), BF16),
                        pltpu.VMEM((rows, 1), F32), pltpu.VMEM((rows, 1), F32), pltpu.VMEM((rows, A_KV_RANK), F32),
                        pltpu.VMEM((2, pps, page, A_KV_RANK), F32), pltpu.VMEM((2, pps, A_ROPE, page), F32),
                        pltpu.SemaphoreType.DMA((2, 2))],
    )
    out = pl.pallas_call(
        functools.partial(_mla_sample_kernel, pps=pps, ppc=ppc, page=page, n_tok=t, nseq=nseq, nj=nj),
        grid_spec=grid_spec,
        out_shape=jax.ShapeDtypeStruct((nseq, T_PAD, A_HEADS * A_V), BF16),
        compiler_params=pltpu.CompilerParams(dimension_semantics=("arbitrary",) * 2, vmem_limit_bytes=VMEM_LIMIT),
        name="mla_sample",
    )(page_table, _pad_tokens(qa, nseq, t), _pad_tokens(lat_new, nseq, t, page), _new_page_t(kr_new, nseq, t, page),
      *consts, cache_lat, cache_kr_t)
    return out[:, :t].reshape(nseq * t, A_HEADS * A_V)


def _dsa_sample_kernel(pt_ref, qi_ref, wi_ref, qb_ref, kin_ref, kbn_ref, vbn_ref, ik_hbm, k_hbm, v_hbm,
                       o_ref, key_scr, thr_scr, jmax_scr, m_scr, l_scr, acc_scr, ik_buf, k_buf, v_buf, sem,
                       *, pps, page, nsel, nbits, n_tok, past, nseq, nj):
    n = pl.program_id(0)
    phase = pl.program_id(1)
    j = pl.program_id(2)
    last = nj - 1
    kt = pps * page
    hp = B_HEADS // 2
    tok = lax.broadcasted_iota(I32, (T_PAD, 1), 0)

    def page_copies(seq, ph, jj, slot):
        cps = []
        for k in range(pps):
            pid = pt_ref[seq, jj * pps + k]
            if ph == 0:
                cps.append(pltpu.make_async_copy(ik_hbm.at[pid], ik_buf.at[slot, k], sem.at[0, slot]))
            else:
                cps.append(pltpu.make_async_copy(k_hbm.at[pid], k_buf.at[slot, k], sem.at[1, slot]))
                cps.append(pltpu.make_async_copy(v_hbm.at[pid], v_buf.at[slot, k], sem.at[2, slot]))
        return cps

    def for_phase(ph_value, fn):
        for ph in range(2):
            pl.when(ph_value == ph)(functools.partial(fn, ph))

    step = (n * 2 + phase) * nj + j
    slot = step & 1

    def start_all(ph, seq, jj, slot):
        for cp in page_copies(seq, ph, jj, slot):
            cp.start()

    @pl.when(step == 0)
    def _():
        start_all(0, n, j, slot)

    @pl.when(step + 1 < nseq * 2 * nj)
    def _():
        wrap = j + 1 == nj
        nph = jnp.where(wrap, 1 - phase, phase)
        nseqi = jnp.where(wrap & (phase == 1), n + 1, n)
        for_phase(nph, functools.partial(start_all, seq=nseqi, jj=jnp.where(wrap, 0, j + 1), slot=1 - slot))

    def wait_all(ph):
        for cp in page_copies(n, ph, j, slot):
            cp.wait()

    for_phase(phase, wait_all)

    def new_visible(width):
        col = lax.broadcasted_iota(I32, (T_PAD, width), 1)
        return (col <= tok) & (col < n_tok)

    @pl.when(phase == 0)
    def _():
        q8 = qi_ref[0].astype(F32)
        qi = jnp.concatenate([q8[:, h * HEAD_BLOCK:h * HEAD_BLOCK + IDX_DIM] for h in range(IDX_HEADS)],
                             axis=0).astype(BF16)
        w = wi_ref[0]

        def score(ikt16):
            s = _dot(qi, ikt16)
            tot = jnp.zeros((T_PAD, s.shape[1]), F32)
            for h in range(IDX_HEADS):
                tot = tot + w[:, h:h + 1] * jnp.maximum(s[h * T_PAD:(h + 1) * T_PAD], 0.0)
            return _order_key(tot)

        ikt = jnp.concatenate([ik_buf[slot, k] for k in range(pps)], axis=1).astype(BF16)
        key_scr[:, pl.ds(pl.multiple_of(j * kt, kt), kt)] = score(ikt)

        @pl.when(j == last)
        def _():
            knew = score(kin_ref[0])
            key_scr[:, past:past + page] = jnp.where(new_visible(page), knew, INT_MIN)
            col = lax.broadcasted_iota(I32, (T_PAD, past + page), 1)

            def count_ge(cand):
                return jnp.sum(jnp.where(key_scr[...] >= cand, 1, 0).astype(I32), axis=-1, keepdims=True)

            def count_eq_below(thr, jj):
                hit = (key_scr[...] == thr) & (col < jj)
                return jnp.sum(jnp.where(hit, 1, 0).astype(I32), axis=-1, keepdims=True)

            thr, jmax = _select_threshold(count_ge, count_eq_below, (T_PAD, 1), past + page, nsel, nbits,
                                          tok < n_tok)
            thr_scr[...] = thr
            jmax_scr[...] = jmax
            m_scr[...] = jnp.full(m_scr.shape, NEG, F32)
            l_scr[...] = jnp.zeros(l_scr.shape, F32)
            acc_scr[...] = jnp.zeros(acc_scr.shape, F32)

    @pl.when(phase == 1)
    def _():
        q8 = qb_ref[0].astype(F32)
        qs = [jnp.concatenate([q8[:, h * HEAD_BLOCK:(h + 1) * HEAD_BLOCK] for h in range(p * hp, (p + 1) * hp)],
                              axis=0).astype(BF16) for p in range(2)]
        thr = thr_scr[...]
        jmax = jmax_scr[...]

        def attend(kt16, vt16, key, col0, extra_mask):
            width = key.shape[1]
            col = col0 + lax.broadcasted_iota(I32, (T_PAD, width), 1)
            sel = (key > thr) | ((key == thr) & (col <= jmax))
            if extra_mask is not None:
                sel = sel & extra_mask
            for p in range(2):
                s = _dot(qs[p], kt16[p * LANES:(p + 1) * LANES]).reshape(hp, T_PAD, width)
                s = jnp.where(sel[None], s, NEG)
                m = m_scr[p]
                m_new = jnp.maximum(m, jnp.max(s, axis=-1, keepdims=True))
                pe = jnp.exp2(s - m_new)
                corr = jnp.exp2(m - m_new)
                l_scr[p] = l_scr[p] * corr + jnp.sum(pe, axis=-1, keepdims=True)
                pv = _dot_nt(pe.reshape(hp * T_PAD, width).astype(BF16), vt16[p * LANES:(p + 1) * LANES])
                acc_scr[p] = acc_scr[p] * corr + pv.reshape(hp, T_PAD, LANES)
                m_scr[p] = m_new

        kt16 = jnp.concatenate([k_buf[slot, k] for k in range(pps)], axis=1).astype(BF16)
        vt16 = jnp.concatenate([v_buf[slot, k] for k in range(pps)], axis=1).astype(BF16)
        off = pl.multiple_of(j * kt, kt)
        attend(kt16, vt16, key_scr[:, pl.ds(off, kt)], off, None)

        @pl.when(j == last)
        def _():
            attend(kbn_ref[0], vbn_ref[0], key_scr[:, past:past + page], past, new_visible(page))
            o_ref[0] = _assemble_b_heads([acc_scr[p] / l_scr[p] for p in range(2)], T_PAD).astype(o_ref.dtype)


def _dsa_sample_call(page_table, qi, wi, qb, ki16, kb16, vb16, cache_ik_t, cache_k_t, cache_v_t, t):
    nseq, npages = page_table.shape
    page = cache_ik_t.shape[2]
    past = npages * page
    pps = min(PAGES_PER_STEP, npages)
    assert npages % pps == 0 and t <= T_PAD
    nj = npages // pps
    nsel = min(TOPK_MAX, (past + t) // 4)
    kvw = B_KV_HEADS * B_HEAD_DIM
    hp = B_HEADS // 2

    new_page = lambda a: _new_page_t(a, nseq, t, page)

    seq_spec = lambda r, w: pl.BlockSpec((1, r, w), lambda n, ph, j, pt: (n, 0, 0))
    hbm_spec = pl.BlockSpec(memory_space=pl.ANY)
    grid_spec = pltpu.PrefetchScalarGridSpec(
        num_scalar_prefetch=1,
        grid=(nseq, 2, nj),
        in_specs=[seq_spec(T_PAD, IDX_HEADS * HEAD_BLOCK), seq_spec(T_PAD, LANES), seq_spec(T_PAD, B_HEADS * HEAD_BLOCK),
                  seq_spec(IDX_DIM, page), seq_spec(kvw, page), seq_spec(kvw, page)] + [hbm_spec] * 3,
        out_specs=seq_spec(T_PAD, B_HEADS * B_HEAD_DIM),
        scratch_shapes=[pltpu.VMEM((T_PAD, past + page), I32), pltpu.VMEM((T_PAD, 1), I32), pltpu.VMEM((T_PAD, 1), I32),
                        pltpu.VMEM((2, hp, T_PAD, 1), F32), pltpu.VMEM((2, hp, T_PAD, 1), F32),
                        pltpu.VMEM((2, hp, T_PAD, LANES), F32),
                        pltpu.VMEM((2, pps, IDX_DIM, page), F32), pltpu.VMEM((2, pps, kvw, page), F32),
                        pltpu.VMEM((2, pps, kvw, page), F32), pltpu.SemaphoreType.DMA((3, 2))],
    )
    out = pl.pallas_call(
        functools.partial(_dsa_sample_kernel, pps=pps, page=page, nsel=nsel, nbits=int(past + page).bit_length(),
                          n_tok=t, past=past, nseq=nseq, nj=nj),
        grid_spec=grid_spec,
        out_shape=jax.ShapeDtypeStruct((nseq, T_PAD, B_HEADS * B_HEAD_DIM), BF16),
        compiler_params=pltpu.CompilerParams(dimension_semantics=("arbitrary",) * 3, vmem_limit_bytes=VMEM_LIMIT),
        name="dsa_sample",
    )(page_table, _pad_tokens(qi, nseq, t), _pad_tokens(wi, nseq, t), _pad_tokens(qb, nseq, t),
      new_page(ki16[:, :IDX_DIM]), new_page(kb16), new_page(vb16),
      cache_ik_t, cache_k_t, cache_v_t)
    return out[:, :t].reshape(nseq * t, B_HEADS * B_HEAD_DIM)


def _merge_kernel(x_ref, oa_ref, ob_ref, gattn_ref, wgate_ref, woa_ref, wob_ref, wout_ref, h_ref):
    x = x_ref[...]
    n = _rms_full(x, gattn_ref[...]).astype(BF16)
    gate = jax.nn.sigmoid(_dot(n, wgate_ref[...]))
    mix = gate[:, :D_MODEL] * _dot(oa_ref[...], woa_ref[...]) + gate[:, D_MODEL:] * _dot(ob_ref[...], wob_ref[...])
    h_ref[...] = x + _dot(mix.astype(BF16), wout_ref[...])


def _mlp_kernel(h_ref, gmlp_ref, wup_ref, wdown_ref, y_ref):
    h = h_ref[...]
    u = jnp.maximum(_dot(_rms_full(h, gmlp_ref[...]).astype(BF16), wup_ref[...]), 0.0)
    y_ref[...] = h + _dot((u * u).astype(BF16), wdown_ref[...])


def _output_call(x, o_a, o_b, g_attn, w_gate, p):
    m = x.shape[0]
    t = min(PROJ_TILE, m)
    row_spec = lambda w: pl.BlockSpec((t, w), lambda i: (i, 0))
    params = pltpu.CompilerParams(dimension_semantics=("arbitrary",), vmem_limit_bytes=VMEM_LIMIT)
    consts = [g_attn, w_gate, p['w_o_a'].astype(BF16), p['w_o_b'].astype(BF16), p['w_out'].astype(BF16)]
    h = pl.pallas_call(
        _merge_kernel,
        grid=(m // t,),
        in_specs=[row_spec(D_MODEL), row_spec(o_a.shape[1]), row_spec(o_b.shape[1])] + [_const_spec(a) for a in consts],
        out_specs=row_spec(D_MODEL),
        out_shape=jax.ShapeDtypeStruct((m, D_MODEL), F32),
        compiler_params=params,
        name="merge",
    )(x, o_a, o_b, *consts)
    consts = [p['g_mlp'].reshape(1, -1), p['w_up'].astype(BF16), p['w_down'].astype(BF16)]
    return pl.pallas_call(
        _mlp_kernel,
        grid=(m // t,),
        in_specs=[row_spec(D_MODEL)] + [_const_spec(a) for a in consts],
        out_specs=row_spec(D_MODEL),
        out_shape=jax.ShapeDtypeStruct((m, D_MODEL), F32),
        compiler_params=params,
        name="mlp",
    )(h, *consts)


def _layer(h_p, h_s, caches, page_table, p):
    n, s, _ = h_p.shape
    nseq, t, _ = h_s.shape
    past = page_table.shape[1] * caches['lat'].shape[1]
    proj_params, extra = _prep_proj_params(p)
    g_attn = proj_params[0]

    xs = h_s.reshape(nseq * t, D_MODEL)
    pos_s = jnp.tile(past + jnp.arange(t), nseq)
    (qa, kcat, va, lat, krope, qb, kb32, kb16, vb32, vb16, qi, ki32, ki16, wi) = _proj_call(
        xs, _rope_tables(pos_s), proj_params)
    o_a = _mla_sample_call(page_table, qa, lat, krope, caches['lat'], _pages_t(caches['kr']), extra, t)
    o_b = _dsa_sample_call(page_table, qi, wi, qb, ki16, kb16, vb16, _pages_t(caches['ik']),
                           _pages_t(caches['k']), _pages_t(caches['v']), t)
    y_s = _output_call(xs, o_a, o_b, g_attn, extra['w_gate'], p).reshape(nseq, t, D_MODEL)
    new_s = (lat.reshape(nseq, t, -1), krope.reshape(nseq, t, -1), kb32.reshape(nseq, t, B_KV_HEADS, B_HEAD_DIM),
             vb32.reshape(nseq, t, B_KV_HEADS, B_HEAD_DIM), ki32.reshape(nseq, t, -1))

    xp = h_p.reshape(n * s, D_MODEL)
    (qa, kcat, va, lat, krope, qb, kb32, kb16, vb32, vb16, qi, ki32, ki16, wi) = _proj_call(
        xp, _rope_tables(jnp.arange(s)), proj_params)
    o_a = _mla_prompt_call(qa, kcat, va, n, s)
    o_b = _dsa_prompt_call(qi, wi, qb, ki16, kb16, vb16, n, s)
    y_p = _output_call(xp, o_a, o_b, g_attn, extra['w_gate'], p).reshape(n, s, D_MODEL)
    new_p = (lat.reshape(n, s, -1), krope.reshape(n, s, -1), kb32.reshape(n, s, B_KV_HEADS, B_HEAD_DIM),
             vb32.reshape(n, s, B_KV_HEADS, B_HEAD_DIM), ki32.reshape(n, s, -1))
    return y_p, y_s, new_p, new_s


def kernel(x_prompt, x_sample, cache_mla_latent, cache_mla_krope, cache_dsa_k, cache_dsa_v, cache_idx_k, page_table, g_attn, w_in, g_q_a, w_uq, g_qn_a, g_qr_a, g_kv_a, g_kr_a, w_uk, g_kn_a, w_uv, w_o_a, g_q_b, g_k_b, w_o_b, w_out, g_mlp, w_up, w_down):
    depth = w_in.shape[0]
    h_p, h_s = x_prompt, x_sample
    new_p, new_s = [], []
    for l in range(depth):
        p = dict(g_attn=g_attn[l], w_in=w_in[l], g_q_a=g_q_a[l], w_uq=w_uq[l], g_qn_a=g_qn_a[l],
                 g_qr_a=g_qr_a[l], g_kv_a=g_kv_a[l], g_kr_a=g_kr_a[l], w_uk=w_uk[l], g_kn_a=g_kn_a[l],
                 w_uv=w_uv[l], w_o_a=w_o_a[l], g_q_b=g_q_b[l], g_k_b=g_k_b[l], w_o_b=w_o_b[l],
                 w_out=w_out[l], g_mlp=g_mlp[l], w_up=w_up[l], w_down=w_down[l])
        caches = dict(lat=cache_mla_latent[l], kr=cache_mla_krope[l], k=cache_dsa_k[l], v=cache_dsa_v[l],
                      ik=cache_idx_k[l])
        h_p, h_s, np_l, ns_l = _layer(h_p, h_s, caches, page_table, p)
        new_p.append(np_l)
        new_s.append(ns_l)
    stack = lambda parts, i: jnp.stack([q[i] for q in parts])
    return (h_p, h_s, *[stack(new_p, i) for i in range(5)], *[stack(new_s, i) for i in range(5)])
```

```python
import functools
import math

import jax
import jax.numpy as jnp
import numpy as np
from jax import lax
from jax.experimental import pallas as pl
from jax.experimental.pallas import tpu as pltpu

F32 = jnp.float32
BF16 = jnp.bfloat16
I32 = jnp.int32

D_MODEL = 1024
A_HEADS = 8
A_NOPE = 64
A_ROPE = 32
A_V = 64
A_Q_RANK = 768
A_KV_RANK = 256
A_SCALE = 1.0 / math.sqrt(A_NOPE + A_ROPE)
B_HEADS = 8
B_KV_HEADS = 4
B_HEAD_DIM = 64
B_SCALE = 1.0 / math.sqrt(B_HEAD_DIM)
IDX_HEADS = 8
IDX_DIM = 64
IDX_SCALE = 1.0 / math.sqrt(IDX_DIM)
TOPK_MAX = 256
D_FF = 4 * D_MODEL
ROPE_THETA = 10000.0
EPS = 1e-6
IN_SIZES = (A_Q_RANK, A_KV_RANK, A_ROPE, B_HEADS * B_HEAD_DIM, B_KV_HEADS * B_HEAD_DIM,
            B_KV_HEADS * B_HEAD_DIM, IDX_HEADS * IDX_DIM, IDX_DIM, IDX_HEADS, 2 * D_MODEL)

LANES = 128
HEAD_BLOCK = LANES
NEG = -1e30
INT_MIN = -(2 ** 31)
VMEM_LIMIT = 56 * 1024 * 1024

LOG2E = math.log2(math.e)

PROJ_TILE = 256
MLA_TQ = 512
DSA_TQ = 256
DSA_TK = 512
COUNT_ROWS = 64
PAGES_PER_STEP = 16
PAGES_PER_CHUNK = 8
T_PAD = 8
SELECT_GROUP = 8


def _dot(a, b):
    return jnp.dot(a, b, preferred_element_type=F32)


def _dot_nt(a, b):
    return lax.dot_general(a, b, (((1,), (1,)), ((), ())), preferred_element_type=F32)


def _rms_full(x, g):
    ms = jnp.mean(x * x, axis=-1, keepdims=True)
    return x * lax.rsqrt(ms + EPS) * g


def _split_bf16(x):
    hi = x.astype(BF16)
    lo = (x - hi.astype(F32)).astype(BF16)
    return hi, lo


def _group_rms(x, gs_ref, inv_ref, e_ref):
    hi, lo = _split_bf16(x * x)
    gs = gs_ref[...]
    ssum = _dot(hi, gs) + _dot(lo, gs)
    r = lax.rsqrt(ssum * inv_ref[...] + EPS)
    rhi, rlo = _split_bf16(r)
    e = e_ref[...]
    return x * (_dot(rhi, e) + _dot(rlo, e))


def _rope(x, tabs, half):
    c, sa, sb = tabs
    outs = []
    for j in range(x.shape[1] // LANES):
        xc = x[:, j * LANES:(j + 1) * LANES]
        outs.append(xc * c + pltpu.roll(xc, LANES - half, 1) * sa + pltpu.roll(xc, half, 1) * sb)
    return outs[0] if len(outs) == 1 else jnp.concatenate(outs, axis=1)


def _proj_kernel(x_ref, gattn_ref, ca_ref, saa_ref, sba_ref, cb_ref, sab_ref, sbb_ref,
                 wqa_ref, gqa_ref, wuq_ref, gsqa_ref, invqa_ref, eqa_ref, gainqa_ref,
                 wckv_ref, gkva_ref, wkr_ref, gainkr_ref,
                 wuk_ref, gskn_ref, invkn_ref, ekn_ref, gainkn_ref, wuv_ref,
                 wqb_ref, gsqb_ref, invqb_ref, eqb_ref, gainqb_ref,
                 wkb_ref, gskb_ref, invkb_ref, ekb_ref, gainkb_ref,
                 wvb_ref, wqi_ref, wki_ref, wwi_ref,
                 qa_o, kcat_o, va_o, lat_o, krope_o, qb_o, kb32_o, kb16_o, vb32_o, vb16_o,
                 qi_o, ki32_o, ki16_o, wi_o):
    n = _rms_full(x_ref[...], gattn_ref[...]).astype(BF16)
    tab_a = (ca_ref[...], saa_ref[...], sba_ref[...])
    tab_b = (cb_ref[...], sab_ref[...], sbb_ref[...])

    qlat = _rms_full(_dot(n, wqa_ref[...]), gqa_ref[...]).astype(BF16)
    qa = _group_rms(_dot(qlat, wuq_ref[...]), gsqa_ref, invqa_ref, eqa_ref) * gainqa_ref[...]
    qa_o[...] = _rope(qa, tab_a, A_ROPE // 2).astype(BF16)

    c = _rms_full(_dot(n, wckv_ref[...]), gkva_ref[...])
    lat_o[...] = c
    c16 = c.astype(BF16)
    kr = _dot(n, wkr_ref[...])
    ms = jnp.sum(kr * kr, axis=-1, keepdims=True) * (1.0 / A_ROPE)
    kr = _rope(kr * lax.rsqrt(ms + EPS) * gainkr_ref[...], tab_a, A_ROPE // 2)
    krope_o[...] = kr[:, A_NOPE:A_NOPE + A_ROPE]
    kn = _group_rms(_dot(c16, wuk_ref[...]), gskn_ref, invkn_ref, ekn_ref) * gainkn_ref[...]
    kcat_o[...] = (kn + jnp.concatenate([kr] * A_HEADS, axis=1)).astype(BF16)
    va_o[...] = _dot(c16, wuv_ref[...]).astype(BF16)

    qb = _group_rms(_dot(n, wqb_ref[...]), gsqb_ref, invqb_ref, eqb_ref) * gainqb_ref[...]
    qb_o[...] = _rope(qb, tab_b, B_HEAD_DIM // 2).astype(BF16)
    kb = _group_rms(_dot(n, wkb_ref[...]), gskb_ref, invkb_ref, ekb_ref) * gainkb_ref[...]
    kb = _rope(kb, tab_b, B_HEAD_DIM // 2)
    kb32_o[...] = kb
    kb16_o[...] = kb.astype(BF16)
    vb = _dot(n, wvb_ref[...])
    vb32_o[...] = vb
    vb16_o[...] = vb.astype(BF16)

    qi = _rope(_dot(n, wqi_ref[...]), tab_b, IDX_DIM // 2) * IDX_SCALE
    qi_o[...] = qi.astype(BF16)
    ki = _rope(_dot(n, wki_ref[...]), tab_b, IDX_DIM // 2)
    ki32_o[...] = ki[:, :IDX_DIM]
    ki16_o[...] = ki.astype(BF16)
    wi_o[...] = _dot(n, wwi_ref[...]) * (IDX_HEADS ** -0.5)


def _const_spec(a):
    nd = a.ndim
    return pl.BlockSpec(a.shape, lambda *_: (0,) * nd)


def _group_mats(width, groups):
    gs = np.zeros((width, LANES), np.float32)
    inv = np.ones((1, LANES), np.float32)
    for j, (start, size) in enumerate(groups):
        gs[start:start + size, j] = 1.0
        inv[0, j] = 1.0 / size
    return jnp.asarray(gs, BF16), jnp.asarray(inv), jnp.asarray(gs.T, BF16)


def _b_half(h):
    return (h // (B_HEADS // B_KV_HEADS)) % 2


def _rope_tables(pos):
    posf = pos.astype(F32)[:, None]

    def cs(d):
        inv = ROPE_THETA ** (-jnp.arange(0, d, 2, dtype=F32) / d)
        ang = posf * inv[None, :]
        return jnp.cos(ang), jnp.sin(ang)

    r = pos.shape[0]
    c16, s16 = cs(A_ROPE)
    one, zero = jnp.ones((r, 1), F32), jnp.zeros((r, 1), F32)
    z16 = jnp.zeros((r, A_ROPE // 2), F32)
    ca = jnp.concatenate([jnp.tile(one, (1, A_NOPE)), c16, c16, jnp.tile(one, (1, 32))], axis=1)
    saa = jnp.concatenate([jnp.tile(zero, (1, A_NOPE)), -s16, z16, jnp.tile(zero, (1, 32))], axis=1)
    sba = jnp.concatenate([jnp.tile(zero, (1, A_NOPE)), z16, s16, jnp.tile(zero, (1, 32))], axis=1)
    c32, s32 = cs(B_HEAD_DIM)
    z32 = jnp.zeros_like(s32)
    cb = jnp.concatenate([c32] * 4, axis=1)
    sab = jnp.concatenate([-s32, z32] * 2, axis=1)
    sbb = jnp.concatenate([z32, s32] * 2, axis=1)
    return ca, saa, sba, cb, sab, sbb


def _prep_proj_params(p):
    offs = np.cumsum((0,) + IN_SIZES)
    w_in = p['w_in']
    w = [w_in[:, offs[i]:offs[i + 1]] for i in range(len(IN_SIZES))]
    w_qa, w_ckv, w_kr, w_qb, w_kb, w_vb, w_qi, w_ki, w_wi, w_gate = w
    d = D_MODEL

    def blocks(wm, heads, dim):
        wm = wm.reshape(wm.shape[0], heads, dim)
        return jnp.pad(wm, ((0, 0), (0, 0), (0, HEAD_BLOCK - dim))).reshape(wm.shape[0], heads * HEAD_BLOCK)

    w_uq = blocks(p['w_uq'], A_HEADS, A_NOPE + A_ROPE)
    w_uk_c = p['w_uk'].reshape(A_KV_RANK, A_HEADS * A_NOPE)
    w_uk = blocks(w_uk_c, A_HEADS, A_NOPE)
    w_uv = p['w_uv'].reshape(A_KV_RANK, A_HEADS * A_V)
    w_kr = jnp.pad(w_kr, ((0, 0), (A_NOPE, LANES - A_NOPE - A_ROPE)))
    qb_cols, gqb = [], []
    zc = jnp.zeros((d, B_HEAD_DIM), F32)
    zg = jnp.zeros((B_HEAD_DIM,), F32)
    for h in range(B_HEADS):
        wh = w_qb[:, h * B_HEAD_DIM:(h + 1) * B_HEAD_DIM]
        qb_cols += [zc, wh] if _b_half(h) else [wh, zc]
        gqb += [zg, p['g_q_b']] if _b_half(h) else [p['g_q_b'], zg]
    w_qb = jnp.concatenate(qb_cols, axis=1)
    w_qi = blocks(w_qi, IDX_HEADS, IDX_DIM)
    w_ki = jnp.pad(w_ki, ((0, 0), (0, LANES - IDX_DIM)))
    w_wi = jnp.pad(w_wi, ((0, 0), (0, LANES - IDX_HEADS)))

    z32 = jnp.zeros((HEAD_BLOCK - A_NOPE - A_ROPE,), F32)
    gain_qa = jnp.tile(jnp.concatenate([p['g_qn_a'], p['g_qr_a'], z32]), A_HEADS) * (A_SCALE * LOG2E)
    gain_kn = jnp.tile(jnp.concatenate([p['g_kn_a'], jnp.zeros((HEAD_BLOCK - A_NOPE,), F32)]), A_HEADS)
    gain_kr = jnp.concatenate([jnp.zeros((A_NOPE,), F32), p['g_kr_a'], z32])
    gain_qb = jnp.concatenate(gqb) * (B_SCALE * LOG2E)
    gain_kb = jnp.tile(p['g_k_b'], B_KV_HEADS)

    qa_groups = []
    for h in range(A_HEADS):
        qa_groups += [(h * HEAD_BLOCK, A_NOPE), (h * HEAD_BLOCK + A_NOPE, A_ROPE)]
    kn_groups = [(h * HEAD_BLOCK, A_NOPE) for h in range(A_HEADS)]
    qb_groups = [(h * HEAD_BLOCK + B_HEAD_DIM * _b_half(h), B_HEAD_DIM) for h in range(B_HEADS)]
    kb_groups = [(g * B_HEAD_DIM, B_HEAD_DIM) for g in range(B_KV_HEADS)]

    row = lambda v: v.reshape(1, -1).astype(F32)
    b16 = lambda m: m.astype(BF16)
    proj = [row(p['g_attn']),
            b16(w_qa), row(p['g_q_a']), b16(w_uq), *_group_mats(A_HEADS * HEAD_BLOCK, qa_groups), row(gain_qa),
            b16(w_ckv), row(p['g_kv_a']), b16(w_kr), row(gain_kr),
            b16(w_uk), *_group_mats(A_HEADS * HEAD_BLOCK, kn_groups), row(gain_kn), b16(w_uv),
            b16(w_qb), *_group_mats(B_HEADS * HEAD_BLOCK, qb_groups), row(gain_qb),
            b16(w_kb), *_group_mats(B_KV_HEADS * B_HEAD_DIM, kb_groups), row(gain_kb),
            b16(w_vb), b16(w_qi), b16(w_ki), b16(w_wi)]
    extra = dict(w_gate=b16(w_gate), w_uk_blocks=b16(w_uk), w_uk_compact=b16(w_uk_c), w_uv=b16(w_uv),
                 g_kn=row(p['g_kn_a']))
    return proj, extra


def _proj_call(x, tables, proj_params):
    m = x.shape[0]
    t = min(PROJ_TILE, m)
    assert m % t == 0 and tables[0].shape[0] % t == 0
    nt = tables[0].shape[0] // t
    row_spec = lambda w: pl.BlockSpec((t, w), lambda i: (i, 0))
    tab_spec = pl.BlockSpec((t, LANES), lambda i: (i % nt, 0))
    hb = A_HEADS * HEAD_BLOCK
    outs = [(hb, BF16), (hb, BF16), (A_HEADS * A_V, BF16), (A_KV_RANK, F32), (A_ROPE, F32),
            (B_HEADS * HEAD_BLOCK, BF16), (B_KV_HEADS * B_HEAD_DIM, F32), (B_KV_HEADS * B_HEAD_DIM, BF16),
            (B_KV_HEADS * B_HEAD_DIM, F32), (B_KV_HEADS * B_HEAD_DIM, BF16),
            (IDX_HEADS * HEAD_BLOCK, BF16), (IDX_DIM, F32), (LANES, BF16), (LANES, F32)]
    return pl.pallas_call(
        _proj_kernel,
        grid=(m // t,),
        in_specs=[row_spec(D_MODEL), _const_spec(proj_params[0])] + [tab_spec] * 6
                 + [_const_spec(a) for a in proj_params[1:]],
        out_specs=[row_spec(w) for w, _ in outs],
        out_shape=[jax.ShapeDtypeStruct((m, w), dt) for w, dt in outs],
        compiler_params=pltpu.CompilerParams(dimension_semantics=("arbitrary",), vmem_limit_bytes=VMEM_LIMIT),
        name="proj",
    )(x, proj_params[0], *tables, *proj_params[1:])


def _mla_prompt_kernel(q_ref, k_ref, v_ref, o_ref, *, tq):
    i = pl.program_id(2)
    rows = lax.broadcasted_iota(I32, (tq, tq), 0)
    cols = lax.broadcasted_iota(I32, (tq, tq), 1)
    causal = cols <= rows
    lane = lax.broadcasted_iota(I32, (tq, LANES), 1)
    qs = [q_ref[:, hh * HEAD_BLOCK:(hh + 1) * HEAD_BLOCK] for hh in range(2)]

    def step(j, carry, masked):
        off = pl.multiple_of(j * tq, tq)
        v = v_ref[pl.ds(off, tq), :]
        new = []
        for hh in range(2):
            m, l, acc = carry[hh]
            s = _dot_nt(qs[hh], k_ref[pl.ds(off, tq), hh * HEAD_BLOCK:(hh + 1) * HEAD_BLOCK])
            if masked:
                s = jnp.where(causal, s, NEG)
            m_new = jnp.maximum(m, jnp.max(s, axis=-1, keepdims=True))
            p = jnp.exp2(s - m_new)
            corr = jnp.exp2(m - m_new)
            l = l * corr + jnp.sum(p, axis=-1, keepdims=True)
            new.append((m_new, l, acc * corr + _dot(p.astype(BF16), v)))
        return tuple(new)

    init = tuple((jnp.full((tq, 1), NEG, F32), jnp.zeros((tq, 1), F32), jnp.zeros((tq, LANES), F32))
                 for _ in range(2))
    carry = lax.fori_loop(0, i, functools.partial(step, masked=False), init)
    res = step(i, carry, masked=True)
    outs = [res[hh][2] / res[hh][1] for hh in range(2)]
    o_ref[...] = jnp.where(lane < A_V, outs[0], outs[1]).astype(o_ref.dtype)


def _mla_prompt_call(qa, kcat, va, n, s):
    tq = min(MLA_TQ, s)
    nq = s // tq
    pairs = A_HEADS // 2
    return pl.pallas_call(
        functools.partial(_mla_prompt_kernel, tq=tq),
        grid=(n, pairs, nq),
        in_specs=[pl.BlockSpec((tq, 2 * HEAD_BLOCK), lambda b, h, i: (b * nq + i, h)),
                  pl.BlockSpec((s, 2 * HEAD_BLOCK), lambda b, h, i: (b, h)),
                  pl.BlockSpec((s, 2 * A_V), lambda b, h, i: (b, h))],
        out_specs=pl.BlockSpec((tq, 2 * A_V), lambda b, h, i: (b * nq + i, h)),
        out_shape=jax.ShapeDtypeStruct((n * s, A_HEADS * A_V), BF16),
        compiler_params=pltpu.CompilerParams(dimension_semantics=("arbitrary",) * 3, vmem_limit_bytes=VMEM_LIMIT),
        name="mla_prompt",
    )(qa, kcat, va)


def _order_key(score):
    bits = pltpu.bitcast(score, I32)
    return bits ^ ((bits >> 31) & 0x7FFFFFFF)


def _select_threshold(count_ge, count_eq_below, shape, n_all, nsel, nbits, row_ok):
    c0 = count_ge(jnp.zeros(shape, I32))
    thr = jnp.where(c0 >= nsel, 0, INT_MIN).astype(I32)
    n_ge = jnp.where(c0 >= nsel, c0, n_all).astype(I32)

    def bit_body(p, carry):
        thr, n_ge = carry
        cand = thr + lax.shift_left(jnp.int32(1), 30 - p)
        cnt = count_ge(cand)
        ok = cnt >= nsel
        return jnp.where(ok, cand, thr), jnp.where(ok, cnt, n_ge)

    thr, n_ge = lax.fori_loop(0, 31, bit_body, (thr, n_ge))
    tie = (n_ge > nsel) & (thr > INT_MIN) & row_ok
    all_cols = jnp.full(shape, 2 ** 30, I32)

    def resolve():
        need = nsel - jnp.where(thr == 2 ** 31 - 1, 0, count_ge(thr + 1))

        def jbit(p, jmax):
            cand = jmax | lax.shift_left(jnp.int32(1), nbits - 1 - p)
            return jnp.where(count_eq_below(thr, cand) < need, cand, jmax)
        jmax = lax.fori_loop(0, nbits, jbit, jnp.zeros(shape, I32))
        return jnp.where(tie, jmax, all_cols)

    jmax = lax.cond(jnp.max(tie.astype(I32)) > 0, resolve, lambda: all_cols)
    return thr, jmax


def _dsa_prompt_kernel(qi_ref, wi_ref, qb_ref, ki_ref, kb_ref, vb_ref, o_ref, key_scr, *, tq, tk, nsel, nbits):
    i = pl.program_id(1)
    q0 = i * tq
    nch = q0 // tk + 1
    row_g = q0 + lax.broadcasted_iota(I32, (tq, tk), 0)
    col_l = lax.broadcasted_iota(I32, (tq, tk), 1)

    qi = jnp.concatenate([qi_ref[:, h * HEAD_BLOCK:(h + 1) * HEAD_BLOCK] for h in range(IDX_HEADS)], axis=0)
    w = wi_ref[...]
    wcol = [w[:, h:h + 1] for h in range(IDX_HEADS)]

    def idx_chunk(c, carry):
        off = pl.multiple_of(c * tk, tk)
        s = _dot_nt(qi, ki_ref[pl.ds(off, tk), :])
        score = jnp.zeros((tq, tk), F32)
        for h in range(IDX_HEADS):
            score = score + wcol[h] * jnp.maximum(s[h * tq:(h + 1) * tq], 0.0)
        key_scr[:, pl.ds(off, tk)] = jnp.where(off + col_l <= row_g, _order_key(score), INT_MIN)
        return carry

    lax.fori_loop(0, nch, idx_chunk, 0)

    rb = min(COUNT_ROWS, tq)
    lane_l = lax.broadcasted_iota(I32, (rb, LANES), 1)

    def count(pred):
        def body(c, accs):
            off = pl.multiple_of(c * tk, tk)
            new = []
            for b in range(tq // rb):
                sl = slice(b * rb, (b + 1) * rb)
                acc = accs[b]
                for jj in range(tk // LANES):
                    k = key_scr[sl, pl.ds(off + jj * LANES, LANES)]
                    acc = acc + jnp.where(pred(k, off + jj * LANES + lane_l, sl), 1, 0).astype(I32)
                new.append(acc)
            return tuple(new)
        accs = lax.fori_loop(0, nch, body, tuple(jnp.zeros((rb, LANES), I32) for _ in range(tq // rb)))
        tot = jnp.sum(jnp.concatenate(accs, axis=0), axis=-1, keepdims=True)
        return jnp.broadcast_to(tot, (tq, LANES))

    count_ge = lambda cand: count(lambda k, col, sl: k >= cand[sl])
    count_eq_below = lambda thr, j: count(lambda k, col, sl: (k == thr[sl]) & (col < j[sl]))
    thr, jmax = _select_threshold(count_ge, count_eq_below, (tq, LANES), nch * tk, nsel, nbits,
                                  lax.broadcasted_iota(I32, (tq, LANES), 0) >= 0)
    thr, jmax = thr[:, :1], jmax[:, :1]

    hp = B_HEADS // 2
    qs = [jnp.concatenate([qb_ref[:, h * HEAD_BLOCK:(h + 1) * HEAD_BLOCK] for h in range(p * hp, (p + 1) * hp)],
                          axis=0) for p in range(2)]

    def att_chunk(c, carry):
        off = pl.multiple_of(c * tk, tk)
        key = key_scr[:, pl.ds(off, tk)]
        col = off + col_l
        sel = ((key > thr) | ((key == thr) & (col <= jmax))) & (col <= row_g)
        new = []
        for p in range(2):
            m, l, acc = carry[p]
            kc = kb_ref[pl.ds(off, tk), p * LANES:(p + 1) * LANES]
            vc = vb_ref[pl.ds(off, tk), p * LANES:(p + 1) * LANES]
            s = jnp.where(sel[None], _dot_nt(qs[p], kc).reshape(hp, tq, tk), NEG)
            m_new = jnp.maximum(m, jnp.max(s, axis=-1, keepdims=True))
            pe = jnp.exp2(s - m_new)
            corr = jnp.exp2(m - m_new)
            l = l * corr + jnp.sum(pe, axis=-1, keepdims=True)
            pv = _dot(pe.reshape(hp * tq, tk).astype(BF16), vc).reshape(hp, tq, LANES)
            new.append((m_new, l, acc * corr + pv))
        return tuple(new)

    init = tuple((jnp.full((hp, tq, 1), NEG, F32), jnp.zeros((hp, tq, 1), F32), jnp.zeros((hp, tq, LANES), F32))
                 for _ in range(2))
    res = lax.fori_loop(0, nch, att_chunk, init)
    o_ref[...] = _assemble_b_heads([res[p][2] / res[p][1] for p in range(2)], tq).astype(o_ref.dtype)


def _assemble_b_heads(accs, rows):
    lane = lax.broadcasted_iota(I32, (rows, LANES), 1)
    swap = lambda a: pltpu.roll(a, B_HEAD_DIM, 1)
    blocks = []
    for p in range(2):
        a = accs[p]
        blocks.append(jnp.where(lane < B_HEAD_DIM, a[0], swap(a[1])))
        blocks.append(jnp.where(lane < B_HEAD_DIM, swap(a[2]), a[3]))
    return jnp.concatenate(blocks, axis=1)


def _dsa_prompt_call(qi, wi, qb, ki16, kb16, vb16, n, s):
    tq = min(DSA_TQ, s)
    tk = min(DSA_TK, s)
    nq = s // tq
    nsel = min(TOPK_MAX, s // 4)
    kvw = B_KV_HEADS * B_HEAD_DIM
    qspec = lambda w: pl.BlockSpec((tq, w), lambda b, i: (b * nq + i, 0))
    kspec = lambda w: pl.BlockSpec((s, w), lambda b, i: (b, 0))
    return pl.pallas_call(
        functools.partial(_dsa_prompt_kernel, tq=tq, tk=tk, nsel=nsel, nbits=int(s).bit_length()),
        grid=(n, nq),
        in_specs=[qspec(IDX_HEADS * HEAD_BLOCK), qspec(LANES), qspec(B_HEADS * HEAD_BLOCK),
                  kspec(LANES), kspec(kvw), kspec(kvw)],
        out_specs=qspec(B_HEADS * B_HEAD_DIM),
        out_shape=jax.ShapeDtypeStruct((n * s, B_HEADS * B_HEAD_DIM), BF16),
        scratch_shapes=[pltpu.VMEM((tq, s), I32)],
        compiler_params=pltpu.CompilerParams(dimension_semantics=("arbitrary",) * 2, vmem_limit_bytes=VMEM_LIMIT),
        name="dsa_prompt",
    )(qi, wi, qb, ki16, kb16, vb16)


def _mla_sample_kernel(pt_ref, q_ref, latn_ref, krn_ref, qmask_ref, wukb_ref, wukt_ref, wuv_ref,
                       hmask_ref, place_ref, lat_hbm, kr_hbm, o_ref, lhs_scr, qr_scr, m_scr, l_scr, acc_scr,
                       lat_buf, kr_buf, sem, *, pps, ppc, page, n_tok, nseq, nj):
    n = pl.program_id(0)
    j = pl.program_id(1)
    rows = A_HEADS * T_PAD
    knw = A_HEADS * A_NOPE

    def page_copies(seq, jj, slot):
        cps = []
        for k in range(pps):
            pid = pt_ref[seq, jj * pps + k]
            cps.append(pltpu.make_async_copy(lat_hbm.at[pid], lat_buf.at[slot, k], sem.at[0, slot]))
            cps.append(pltpu.make_async_copy(kr_hbm.at[pid], kr_buf.at[slot, k], sem.at[1, slot]))
        return cps

    step = n * nj + j
    slot = step & 1

    @pl.when(step == 0)
    def _():
        for cp in page_copies(n, j, slot):
            cp.start()

    @pl.when(step + 1 < nseq * nj)
    def _():
        wrap = j + 1 == nj
        for cp in page_copies(jnp.where(wrap, n + 1, n), jnp.where(wrap, 0, j + 1), 1 - slot):
            cp.start()

    for cp in page_copies(n, j, slot):
        cp.wait()

    @pl.when(j == 0)
    def _():
        q8 = q_ref[0].astype(F32)
        qpad = (jnp.concatenate([q8] * A_HEADS, axis=0) * qmask_ref[...]).astype(BF16)
        lhs_scr[:knw, :] = wukt_ref[...]
        lhs_scr[knw:, :] = _dot_nt(qpad, wukb_ref[...]).astype(BF16)
        qr = jnp.concatenate([q8[:, h * HEAD_BLOCK:(h + 1) * HEAD_BLOCK] for h in range(A_HEADS)], axis=0)
        qr_scr[...] = _dot(qr.astype(BF16), place_ref[...]).astype(BF16)
        m_scr[...] = jnp.full((rows, 1), NEG, F32)
        l_scr[...] = jnp.zeros((rows, 1), F32)
        acc_scr[...] = jnp.zeros((rows, A_KV_RANK), F32)

    def partial(c, krt, mask):
        nk = c.shape[0]
        c16 = c.astype(BF16)
        both = _dot_nt(lhs_scr[...], c16)
        knt = both[:knw]
        msq = jnp.sum((knt * knt).reshape(A_HEADS, A_NOPE, nk), axis=1) * (1.0 / A_NOPE)
        r = lax.rsqrt(msq + EPS)
        r = jnp.broadcast_to(r[:, None, :], (A_HEADS, T_PAD, nk)).reshape(rows, nk)
        s = both[knw:] * r + _dot(qr_scr[...], krt.astype(BF16))
        if mask is not None:
            s = jnp.where(mask, s, NEG)
        mg = jnp.max(s, axis=-1, keepdims=True)
        p = jnp.exp2(s - mg)
        return mg, jnp.sum(p, axis=-1, keepdims=True), _dot(p.astype(BF16), c16)

    def merge(parts):
        m = m_scr[...]
        m_new = m
        for mg, _, _ in parts:
            m_new = jnp.maximum(m_new, mg)
        corr = jnp.exp2(m - m_new)
        l = l_scr[...] * corr
        acc = acc_scr[...] * corr
        for mg, lg, pv in parts:
            w = jnp.exp2(mg - m_new)
            l = l + lg * w
            acc = acc + pv * w
        m_scr[...] = m_new
        l_scr[...] = l
        acc_scr[...] = acc

    parts = []
    for g in range(pps // ppc):
        c = jnp.concatenate([lat_buf[slot, g * ppc + k] for k in range(ppc)], axis=0)
        krt = jnp.concatenate([kr_buf[slot, g * ppc + k] for k in range(ppc)], axis=1)
        parts.append(partial(c, krt, None))
    merge(parts)

    @pl.when(j == pl.num_programs(1) - 1)
    def _():
        r = lax.broadcasted_iota(I32, (rows, page), 0) & (T_PAD - 1)
        col = lax.broadcasted_iota(I32, (rows, page), 1)
        merge([partial(latn_ref[0], krn_ref[0], (col <= r) & (col < n_tok))])
        lat_out = (acc_scr[...] / l_scr[...]).astype(BF16)
        full = _dot(lat_out, wuv_ref[...])
        hm = hmask_ref[...]
        out = jnp.zeros((T_PAD, A_HEADS * A_V), F32)
        for h in range(A_HEADS):
            out = out + full[h * T_PAD:(h + 1) * T_PAD] * hm[h:h + 1]
        o_ref[0] = out.astype(o_ref.dtype)


def _pad_tokens(a, nseq, t, rows=T_PAD):
    a = a.reshape(nseq, t, a.shape[-1])
    return jnp.pad(a, ((0, 0), (0, rows - t), (0, 0)))


def _new_page_t(a, nseq, t, page):
    return jnp.swapaxes(_pad_tokens(a, nseq, t, page), 1, 2)


def _pages_t(cache):
    pool, page = cache.shape[:2]
    return jnp.moveaxis(cache, 1, -1).reshape(pool, -1, page)


def _mla_sample_call(page_table, qa, lat_new, kr_new, cache_lat, cache_kr_t, extra, t):
    nseq, npages = page_table.shape
    page = cache_lat.shape[1]
    pps = min(PAGES_PER_STEP, npages)
    ppc = min(PAGES_PER_CHUNK, pps)
    assert npages % pps == 0 and pps % ppc == 0
    rows = A_HEADS * T_PAD
    hb = A_HEADS * HEAD_BLOCK

    qmask = np.zeros((rows, hb), np.float32)
    hmask = np.zeros((A_HEADS, A_HEADS * A_V), np.float32)
    for h in range(A_HEADS):
        qmask[h * T_PAD:(h + 1) * T_PAD, h * HEAD_BLOCK:h * HEAD_BLOCK + A_NOPE] = 1.0
        hmask[h, h * A_V:(h + 1) * A_V] = 1.0
    gkn_blocks = jnp.tile(jnp.pad(extra['g_kn'], ((0, 0), (0, HEAD_BLOCK - A_NOPE))), (1, A_HEADS))
    qmask = jnp.asarray(qmask) * gkn_blocks
    place = np.zeros((HEAD_BLOCK, A_ROPE), np.float32)
    place[A_NOPE:A_NOPE + A_ROPE, :] = np.eye(A_ROPE, dtype=np.float32)
    consts = [qmask, extra['w_uk_blocks'], extra['w_uk_compact'].T, extra['w_uv'],
              jnp.asarray(hmask), jnp.asarray(place, BF16)]

    seq_spec = lambda r, w: pl.BlockSpec((1, r, w), lambda n, j, pt: (n, 0, 0))
    cspec = lambda a: pl.BlockSpec(a.shape, lambda n, j, pt: (0,) * a.ndim)
    hbm_spec = pl.BlockSpec(memory_space=pl.ANY)
    nj = npages // pps
    grid_spec = pltpu.PrefetchScalarGridSpec(
        num_scalar_prefetch=1,
        grid=(nseq, nj),
        in_specs=[seq_spec(T_PAD, hb), seq_spec(page, A_KV_RANK), seq_spec(A_ROPE, page)] + [cspec(a) for a in consts]
                 + [hbm_spec, hbm_spec],
        out_specs=seq_spec(T_PAD, A_HEADS * A_V),
        scratch_shapes=[pltpu.VMEM((A_HEADS * A_NOPE + rows, A_KV_RANK), BF16), pltpu.VMEM((rows, A_ROPE), BF16),
                        pltpu.VMEM((rows, 1), F32), pltpu.VMEM((rows, 1), F32), pltpu.VMEM((rows, A_KV_RANK), F32),
                        pltpu.VMEM((2, pps, page, A_KV_RANK), F32), pltpu.VMEM((2, pps, A_ROPE, page), F32),
                        pltpu.SemaphoreType.DMA((2, 2))],
    )
    out = pl.pallas_call(
        functools.partial(_mla_sample_kernel, pps=pps, ppc=ppc, page=page, n_tok=t, nseq=nseq, nj=nj),
        grid_spec=grid_spec,
        out_shape=jax.ShapeDtypeStruct((nseq, T_PAD, A_HEADS * A_V), BF16),
        compiler_params=pltpu.CompilerParams(dimension_semantics=("arbitrary",) * 2, vmem_limit_bytes=VMEM_LIMIT),
        name="mla_sample",
    )(page_table, _pad_tokens(qa, nseq, t), _pad_tokens(lat_new, nseq, t, page), _new_page_t(kr_new, nseq, t, page),
      *consts, cache_lat, cache_kr_t)
    return out[:, :t].reshape(nseq * t, A_HEADS * A_V)


def _dsa_sample_kernel(pt_ref, qi_ref, wi_ref, qb_ref, kin_ref, kbn_ref, vbn_ref, ik_hbm, k_hbm, v_hbm,
                       o_ref, key_scr, thr_scr, jmax_scr, m_scr, l_scr, acc_scr, ik_buf, k_buf, v_buf, sem,
                       *, pps, page, nsel, nbits, n_tok, past, ngroups, gsz, nj):
    g = pl.program_id(0)
    phase = pl.program_id(1)
    si = pl.program_id(2)
    j = pl.program_id(3)
    n = g * gsz + si
    last = nj - 1
    kt = pps * page
    hp = B_HEADS // 2
    tok = lax.broadcasted_iota(I32, (T_PAD, 1), 0)
    row0 = pl.multiple_of(si * T_PAD, T_PAD)

    def page_copies(seq, ph, jj, slot):
        cps = []
        for k in range(pps):
            pid = pt_ref[seq, jj * pps + k]
            if ph == 0:
                cps.append(pltpu.make_async_copy(ik_hbm.at[pid], ik_buf.at[slot, k], sem.at[0, slot]))
            else:
                cps.append(pltpu.make_async_copy(k_hbm.at[pid], k_buf.at[slot, k], sem.at[1, slot]))
                cps.append(pltpu.make_async_copy(v_hbm.at[pid], v_buf.at[slot, k], sem.at[2, slot]))
        return cps

    def for_phase(ph_value, fn):
        for ph in range(2):
            pl.when(ph_value == ph)(functools.partial(fn, ph))

    step = ((g * 2 + phase) * gsz + si) * nj + j
    slot = step & 1

    def start_all(ph, seq, jj, slot):
        for cp in page_copies(seq, ph, jj, slot):
            cp.start()

    @pl.when(step == 0)
    def _():
        start_all(0, n, j, slot)

    @pl.when(step + 1 < ngroups * 2 * gsz * nj)
    def _():
        wrap_j = j + 1 == nj
        wrap_s = wrap_j & (si + 1 == gsz)
        next_si = jnp.where(wrap_j, jnp.where(wrap_s, 0, si + 1), si)
        next_ph = jnp.where(wrap_s, 1 - phase, phase)
        next_g = jnp.where(wrap_s & (phase == 1), g + 1, g)
        for_phase(next_ph, functools.partial(start_all, seq=next_g * gsz + next_si,
                                             jj=jnp.where(wrap_j, 0, j + 1), slot=1 - slot))

    def wait_all(ph):
        for cp in page_copies(n, ph, j, slot):
            cp.wait()

    for_phase(phase, wait_all)

    def new_visible(width):
        col = lax.broadcasted_iota(I32, (T_PAD, width), 1)
        return (col <= tok) & (col < n_tok)

    @pl.when(phase == 0)
    def _():
        q8 = qi_ref[0].astype(F32)
        qi = jnp.concatenate([q8[:, h * HEAD_BLOCK:h * HEAD_BLOCK + IDX_DIM] for h in range(IDX_HEADS)],
                             axis=0).astype(BF16)
        w = wi_ref[0]

        def score(ikt16):
            s = _dot(qi, ikt16)
            tot = jnp.zeros((T_PAD, s.shape[1]), F32)
            for h in range(IDX_HEADS):
                tot = tot + w[:, h:h + 1] * jnp.maximum(s[h * T_PAD:(h + 1) * T_PAD], 0.0)
            return _order_key(tot)

        ikt = jnp.concatenate([ik_buf[slot, k] for k in range(pps)], axis=1).astype(BF16)
        key_scr[pl.ds(row0, T_PAD), pl.ds(pl.multiple_of(j * kt, kt), kt)] = score(ikt)

        @pl.when(j == last)
        def _():
            knew = score(kin_ref[0])
            key_scr[pl.ds(row0, T_PAD), past:past + page] = jnp.where(new_visible(page), knew, INT_MIN)

        @pl.when((j == last) & (si == gsz - 1))
        def _():
            grows = gsz * T_PAD
            col = lax.broadcasted_iota(I32, (grows, past + page), 1)
            row_tok = lax.broadcasted_iota(I32, (grows, 1), 0) & (T_PAD - 1)

            def count_ge(cand):
                return jnp.sum(jnp.where(key_scr[...] >= cand, 1, 0).astype(I32), axis=-1, keepdims=True)

            def count_eq_below(thr, jj):
                hit = (key_scr[...] == thr) & (col < jj)
                return jnp.sum(jnp.where(hit, 1, 0).astype(I32), axis=-1, keepdims=True)

            thr, jmax = _select_threshold(count_ge, count_eq_below, (grows, 1), past + page, nsel, nbits,
                                          row_tok < n_tok)
            thr_scr[...] = thr
            jmax_scr[...] = jmax

    @pl.when(phase == 1)
    def _():
        @pl.when(j == 0)
        def _():
            m_scr[...] = jnp.full(m_scr.shape, NEG, F32)
            l_scr[...] = jnp.zeros(l_scr.shape, F32)
            acc_scr[...] = jnp.zeros(acc_scr.shape, F32)

        q8 = qb_ref[0].astype(F32)
        qs = [jnp.concatenate([q8[:, h * HEAD_BLOCK:(h + 1) * HEAD_BLOCK] for h in range(p * hp, (p + 1) * hp)],
                              axis=0).astype(BF16) for p in range(2)]
        thr = thr_scr[pl.ds(row0, T_PAD), :]
        jmax = jmax_scr[pl.ds(row0, T_PAD), :]

        def attend(kt16, vt16, key, col0, extra_mask):
            width = key.shape[1]
            col = col0 + lax.broadcasted_iota(I32, (T_PAD, width), 1)
            sel = (key > thr) | ((key == thr) & (col <= jmax))
            if extra_mask is not None:
                sel = sel & extra_mask
            for p in range(2):
                s = _dot(qs[p], kt16[p * LANES:(p + 1) * LANES]).reshape(hp, T_PAD, width)
                s = jnp.where(sel[None], s, NEG)
                m = m_scr[p]
                m_new = jnp.maximum(m, jnp.max(s, axis=-1, keepdims=True))
                pe = jnp.exp2(s - m_new)
                corr = jnp.exp2(m - m_new)
                l_scr[p] = l_scr[p] * corr + jnp.sum(pe, axis=-1, keepdims=True)
                pv = _dot_nt(pe.reshape(hp * T_PAD, width).astype(BF16), vt16[p * LANES:(p + 1) * LANES])
                acc_scr[p] = acc_scr[p] * corr + pv.reshape(hp, T_PAD, LANES)
                m_scr[p] = m_new

        kt16 = jnp.concatenate([k_buf[slot, k] for k in range(pps)], axis=1).astype(BF16)
        vt16 = jnp.concatenate([v_buf[slot, k] for k in range(pps)], axis=1).astype(BF16)
        off = pl.multiple_of(j * kt, kt)
        attend(kt16, vt16, key_scr[pl.ds(row0, T_PAD), pl.ds(off, kt)], off, None)

        @pl.when(j == last)
        def _():
            attend(kbn_ref[0], vbn_ref[0], key_scr[pl.ds(row0, T_PAD), past:past + page], past, new_visible(page))
            o_ref[0] = _assemble_b_heads([acc_scr[p] / l_scr[p] for p in range(2)], T_PAD).astype(o_ref.dtype)


def _dsa_sample_call(page_table, qi, wi, qb, ki16, kb16, vb16, cache_ik_t, cache_k_t, cache_v_t, t):
    nseq, npages = page_table.shape
    page = cache_ik_t.shape[2]
    past = npages * page
    pps = min(PAGES_PER_STEP, npages)
    assert npages % pps == 0 and t <= T_PAD
    nj = npages // pps
    nsel = min(TOPK_MAX, (past + t) // 4)
    kvw = B_KV_HEADS * B_HEAD_DIM
    hp = B_HEADS // 2

    new_page = lambda a: _new_page_t(a, nseq, t, page)
    gsz = math.gcd(SELECT_GROUP, nseq)
    ngroups = nseq // gsz

    seq_spec = lambda r, w: pl.BlockSpec((1, r, w), lambda g, ph, si, j, pt: (g * gsz + si, 0, 0))
    out_spec = pl.BlockSpec((1, T_PAD, B_HEADS * B_HEAD_DIM), lambda g, ph, si, j, pt: (g * gsz + si * ph, 0, 0))
    hbm_spec = pl.BlockSpec(memory_space=pl.ANY)
    grid_spec = pltpu.PrefetchScalarGridSpec(
        num_scalar_prefetch=1,
        grid=(ngroups, 2, gsz, nj),
        in_specs=[seq_spec(T_PAD, IDX_HEADS * HEAD_BLOCK), seq_spec(T_PAD, LANES), seq_spec(T_PAD, B_HEADS * HEAD_BLOCK),
                  seq_spec(IDX_DIM, page), seq_spec(kvw, page), seq_spec(kvw, page)] + [hbm_spec] * 3,
        out_specs=out_spec,
        scratch_shapes=[pltpu.VMEM((gsz * T_PAD, past + page), I32), pltpu.VMEM((gsz * T_PAD, 1), I32),
                        pltpu.VMEM((gsz * T_PAD, 1), I32),
                        pltpu.VMEM((2, hp, T_PAD, 1), F32), pltpu.VMEM((2, hp, T_PAD, 1), F32),
                        pltpu.VMEM((2, hp, T_PAD, LANES), F32),
                        pltpu.VMEM((2, pps, IDX_DIM, page), F32), pltpu.VMEM((2, pps, kvw, page), F32),
                        pltpu.VMEM((2, pps, kvw, page), F32), pltpu.SemaphoreType.DMA((3, 2))],
    )
    out = pl.pallas_call(
        functools.partial(_dsa_sample_kernel, pps=pps, page=page, nsel=nsel, nbits=int(past + page).bit_length(),
                          n_tok=t, past=past, ngroups=ngroups, gsz=gsz, nj=nj),
        grid_spec=grid_spec,
        out_shape=jax.ShapeDtypeStruct((nseq, T_PAD, B_HEADS * B_HEAD_DIM), BF16),
        compiler_params=pltpu.CompilerParams(dimension_semantics=("arbitrary",) * 4, vmem_limit_bytes=VMEM_LIMIT),
        name="dsa_sample",
    )(page_table, _pad_tokens(qi, nseq, t), _pad_tokens(wi, nseq, t), _pad_tokens(qb, nseq, t),
      new_page(ki16[:, :IDX_DIM]), new_page(kb16), new_page(vb16),
      cache_ik_t, cache_k_t, cache_v_t)
    return out[:, :t].reshape(nseq * t, B_HEADS * B_HEAD_DIM)


def _merge_kernel(x_ref, oa_ref, ob_ref, gattn_ref, wgate_ref, woa_ref, wob_ref, wout_ref, h_ref):
    x = x_ref[...]
    n = _rms_full(x, gattn_ref[...]).astype(BF16)
    gate = jax.nn.sigmoid(_dot(n, wgate_ref[...]))
    mix = gate[:, :D_MODEL] * _dot(oa_ref[...], woa_ref[...]) + gate[:, D_MODEL:] * _dot(ob_ref[...], wob_ref[...])
    h_ref[...] = x + _dot(mix.astype(BF16), wout_ref[...])


def _mlp_kernel(h_ref, gmlp_ref, wup_ref, wdown_ref, y_ref):
    h = h_ref[...]
    u = jnp.maximum(_dot(_rms_full(h, gmlp_ref[...]).astype(BF16), wup_ref[...]), 0.0)
    y_ref[...] = h + _dot((u * u).astype(BF16), wdown_ref[...])


def _output_call(x, o_a, o_b, g_attn, w_gate, p):
    m = x.shape[0]
    t = min(PROJ_TILE, m)
    row_spec = lambda w: pl.BlockSpec((t, w), lambda i: (i, 0))
    params = pltpu.CompilerParams(dimension_semantics=("arbitrary",), vmem_limit_bytes=VMEM_LIMIT)
    consts = [g_attn, w_gate, p['w_o_a'].astype(BF16), p['w_o_b'].astype(BF16), p['w_out'].astype(BF16)]
    h = pl.pallas_call(
        _merge_kernel,
        grid=(m // t,),
        in_specs=[row_spec(D_MODEL), row_spec(o_a.shape[1]), row_spec(o_b.shape[1])] + [_const_spec(a) for a in consts],
        out_specs=row_spec(D_MODEL),
        out_shape=jax.ShapeDtypeStruct((m, D_MODEL), F32),
        compiler_params=params,
        name="merge",
    )(x, o_a, o_b, *consts)
    consts = [p['g_mlp'].reshape(1, -1), p['w_up'].astype(BF16), p['w_down'].astype(BF16)]
    return pl.pallas_call(
        _mlp_kernel,
        grid=(m // t,),
        in_specs=[row_spec(D_MODEL)] + [_const_spec(a) for a in consts],
        out_specs=row_spec(D_MODEL),
        out_shape=jax.ShapeDtypeStruct((m, D_MODEL), F32),
        compiler_params=params,
        name="mlp",
    )(h, *consts)


def _layer(h_p, h_s, caches, page_table, p):
    n, s, _ = h_p.shape
    nseq, t, _ = h_s.shape
    past = page_table.shape[1] * caches['lat'].shape[1]
    proj_params, extra = _prep_proj_params(p)
    g_attn = proj_params[0]

    xs = h_s.reshape(nseq * t, D_MODEL)
    pos_s = jnp.tile(past + jnp.arange(t), nseq)
    (qa, kcat, va, lat, krope, qb, kb32, kb16, vb32, vb16, qi, ki32, ki16, wi) = _proj_call(
        xs, _rope_tables(pos_s), proj_params)
    o_a = _mla_sample_call(page_table, qa, lat, krope, caches['lat'], _pages_t(caches['kr']), extra, t)
    o_b = _dsa_sample_call(page_table, qi, wi, qb, ki16, kb16, vb16, _pages_t(caches['ik']),
                           _pages_t(caches['k']), _pages_t(caches['v']), t)
    y_s = _output_call(xs, o_a, o_b, g_attn, extra['w_gate'], p).reshape(nseq, t, D_MODEL)
    new_s = (lat.reshape(nseq, t, -1), krope.reshape(nseq, t, -1), kb32.reshape(nseq, t, B_KV_HEADS, B_HEAD_DIM),
             vb32.reshape(nseq, t, B_KV_HEADS, B_HEAD_DIM), ki32.reshape(nseq, t, -1))

    xp = h_p.reshape(n * s, D_MODEL)
    (qa, kcat, va, lat, krope, qb, kb32, kb16, vb32, vb16, qi, ki32, ki16, wi) = _proj_call(
        xp, _rope_tables(jnp.arange(s)), proj_params)
    o_a = _mla_prompt_call(qa, kcat, va, n, s)
    o_b = _dsa_prompt_call(qi, wi, qb, ki16, kb16, vb16, n, s)
    y_p = _output_call(xp, o_a, o_b, g_attn, extra['w_gate'], p).reshape(n, s, D_MODEL)
    new_p = (lat.reshape(n, s, -1), krope.reshape(n, s, -1), kb32.reshape(n, s, B_KV_HEADS, B_HEAD_DIM),
             vb32.reshape(n, s, B_KV_HEADS, B_HEAD_DIM), ki32.reshape(n, s, -1))
    return y_p, y_s, new_p, new_s


def kernel(x_prompt, x_sample, cache_mla_latent, cache_mla_krope, cache_dsa_k, cache_dsa_v, cache_idx_k, page_table, g_attn, w_in, g_q_a, w_uq, g_qn_a, g_qr_a, g_kv_a, g_kr_a, w_uk, g_kn_a, w_uv, w_o_a, g_q_b, g_k_b, w_o_b, w_out, g_mlp, w_up, w_down):
    depth = w_in.shape[0]
    h_p, h_s = x_prompt, x_sample
    new_p, new_s = [], []
    for l in range(depth):
        p = dict(g_attn=g_attn[l], w_in=w_in[l], g_q_a=g_q_a[l], w_uq=w_uq[l], g_qn_a=g_qn_a[l],
                 g_qr_a=g_qr_a[l], g_kv_a=g_kv_a[l], g_kr_a=g_kr_a[l], w_uk=w_uk[l], g_kn_a=g_kn_a[l],
                 w_uv=w_uv[l], w_o_a=w_o_a[l], g_q_b=g_q_b[l], g_k_b=g_k_b[l], w_o_b=w_o_b[l],
                 w_out=w_out[l], g_mlp=g_mlp[l], w_up=w_up[l], w_down=w_down[l])
        caches = dict(lat=cache_mla_latent[l], kr=cache_mla_krope[l], k=cache_dsa_k[l], v=cache_dsa_v[l],
                      ik=cache_idx_k[l])
        h_p, h_s, np_l, ns_l = _layer(h_p, h_s, caches, page_table, p)
        new_p.append(np_l)
        new_s.append(ns_l)
    stack = lambda parts, i: jnp.stack([q[i] for q in parts])
    return (h_p, h_s, *[stack(new_p, i) for i in range(5)], *[stack(new_s, i) for i in range(5)])
```

```python
import functools
import math

import jax
import jax.numpy as jnp
import numpy as np
from jax import lax
from jax.experimental import pallas as pl
from jax.experimental.pallas import tpu as pltpu

F32 = jnp.float32
BF16 = jnp.bfloat16
I32 = jnp.int32

D_MODEL = 1024
A_HEADS = 8
A_NOPE = 64
A_ROPE = 32
A_V = 64
A_Q_RANK = 768
A_KV_RANK = 256
A_SCALE = 1.0 / math.sqrt(A_NOPE + A_ROPE)
B_HEADS = 8
B_KV_HEADS = 4
B_HEAD_DIM = 64
B_SCALE = 1.0 / math.sqrt(B_HEAD_DIM)
IDX_HEADS = 8
IDX_DIM = 64
IDX_SCALE = 1.0 / math.sqrt(IDX_DIM)
TOPK_MAX = 256
D_FF = 4 * D_MODEL
ROPE_THETA = 10000.0
EPS = 1e-6
IN_SIZES = (A_Q_RANK, A_KV_RANK, A_ROPE, B_HEADS * B_HEAD_DIM, B_KV_HEADS * B_HEAD_DIM,
            B_KV_HEADS * B_HEAD_DIM, IDX_HEADS * IDX_DIM, IDX_DIM, IDX_HEADS, 2 * D_MODEL)

LANES = 128
HEAD_BLOCK = LANES
NEG = -1e30
INT_MIN = -(2 ** 31)
VMEM_LIMIT = 56 * 1024 * 1024

LOG2E = math.log2(math.e)

PROJ_TILE = 256
MLA_TQ = 1024
DSA_TQ = 256
DSA_TK = 1024
COUNT_ROWS = 64
PAGES_PER_STEP = 32
PAGES_PER_CHUNK = 8
T_PAD = 8
SELECT_GROUP = 8


def _dot(a, b):
    return jnp.dot(a, b, preferred_element_type=F32)


def _dot_nt(a, b):
    return lax.dot_general(a, b, (((1,), (1,)), ((), ())), preferred_element_type=F32)


def _rms_full(x, g):
    ms = jnp.mean(x * x, axis=-1, keepdims=True)
    return x * lax.rsqrt(ms + EPS) * g


def _split_bf16(x):
    hi = x.astype(BF16)
    lo = (x - hi.astype(F32)).astype(BF16)
    return hi, lo


def _group_rms(x, gs_ref, inv_ref, e_ref):
    hi, lo = _split_bf16(x * x)
    gs = gs_ref[...]
    ssum = _dot(hi, gs) + _dot(lo, gs)
    r = lax.rsqrt(ssum * inv_ref[...] + EPS)
    rhi, rlo = _split_bf16(r)
    e = e_ref[...]
    return x * (_dot(rhi, e) + _dot(rlo, e))


def _rope(x, tabs, half):
    c, sa, sb = tabs
    outs = []
    for j in range(x.shape[1] // LANES):
        xc = x[:, j * LANES:(j + 1) * LANES]
        outs.append(xc * c + pltpu.roll(xc, LANES - half, 1) * sa + pltpu.roll(xc, half, 1) * sb)
    return outs[0] if len(outs) == 1 else jnp.concatenate(outs, axis=1)


def _proj_kernel(x_ref, gattn_ref, ca_ref, saa_ref, sba_ref, cb_ref, sab_ref, sbb_ref,
                 wqa_ref, gqa_ref, wuq_ref, gsqa_ref, invqa_ref, eqa_ref, gainqa_ref,
                 wckv_ref, gkva_ref, wkr_ref, gainkr_ref,
                 wuk_ref, gskn_ref, invkn_ref, ekn_ref, gainkn_ref, wuv_ref,
                 wqb_ref, gsqb_ref, invqb_ref, eqb_ref, gainqb_ref,
                 wkb_ref, gskb_ref, invkb_ref, ekb_ref, gainkb_ref,
                 wvb_ref, wqi_ref, wki_ref, wwi_ref,
                 qa_o, kcat_o, va_o, lat_o, krope_o, qb_o, kb32_o, kb16_o, vb32_o, vb16_o,
                 qi_o, ki32_o, ki16_o, wi_o):
    n = _rms_full(x_ref[...], gattn_ref[...]).astype(BF16)
    tab_a = (ca_ref[...], saa_ref[...], sba_ref[...])
    tab_b = (cb_ref[...], sab_ref[...], sbb_ref[...])

    qlat = _rms_full(_dot(n, wqa_ref[...]), gqa_ref[...]).astype(BF16)
    qa = _group_rms(_dot(qlat, wuq_ref[...]), gsqa_ref, invqa_ref, eqa_ref) * gainqa_ref[...]
    qa_o[...] = _rope(qa, tab_a, A_ROPE // 2).astype(BF16)

    c = _rms_full(_dot(n, wckv_ref[...]), gkva_ref[...])
    lat_o[...] = c
    c16 = c.astype(BF16)
    kr = _dot(n, wkr_ref[...])
    ms = jnp.sum(kr * kr, axis=-1, keepdims=True) * (1.0 / A_ROPE)
    kr = _rope(kr * lax.rsqrt(ms + EPS) * gainkr_ref[...], tab_a, A_ROPE // 2)
    krope_o[...] = kr[:, A_NOPE:A_NOPE + A_ROPE]
    kn = _group_rms(_dot(c16, wuk_ref[...]), gskn_ref, invkn_ref, ekn_ref) * gainkn_ref[...]
    kcat_o[...] = (kn + jnp.concatenate([kr] * A_HEADS, axis=1)).astype(BF16)
    va_o[...] = _dot(c16, wuv_ref[...]).astype(BF16)

    qb = _group_rms(_dot(n, wqb_ref[...]), gsqb_ref, invqb_ref, eqb_ref) * gainqb_ref[...]
    qb_o[...] = _rope(qb, tab_b, B_HEAD_DIM // 2).astype(BF16)
    kb = _group_rms(_dot(n, wkb_ref[...]), gskb_ref, invkb_ref, ekb_ref) * gainkb_ref[...]
    kb = _rope(kb, tab_b, B_HEAD_DIM // 2)
    kb32_o[...] = kb
    kb16_o[...] = kb.astype(BF16)
    vb = _dot(n, wvb_ref[...])
    vb32_o[...] = vb
    vb16_o[...] = vb.astype(BF16)

    qi = _rope(_dot(n, wqi_ref[...]), tab_b, IDX_DIM // 2) * IDX_SCALE
    qi_o[...] = qi.astype(BF16)
    ki = _rope(_dot(n, wki_ref[...]), tab_b, IDX_DIM // 2)
    ki32_o[...] = ki[:, :IDX_DIM]
    ki16_o[...] = ki.astype(BF16)
    wi_o[...] = _dot(n, wwi_ref[...]) * (IDX_HEADS ** -0.5)


def _const_spec(a):
    nd = a.ndim
    return pl.BlockSpec(a.shape, lambda *_: (0,) * nd)


def _group_mats(width, groups):
    gs = np.zeros((width, LANES), np.float32)
    inv = np.ones((1, LANES), np.float32)
    for j, (start, size) in enumerate(groups):
        gs[start:start + size, j] = 1.0
        inv[0, j] = 1.0 / size
    return jnp.asarray(gs, BF16), jnp.asarray(inv), jnp.asarray(gs.T, BF16)


def _b_half(h):
    return (h // (B_HEADS // B_KV_HEADS)) % 2


def _rope_tables(pos):
    posf = pos.astype(F32)[:, None]

    def cs(d):
        inv = ROPE_THETA ** (-jnp.arange(0, d, 2, dtype=F32) / d)
        ang = posf * inv[None, :]
        return jnp.cos(ang), jnp.sin(ang)

    r = pos.shape[0]
    c16, s16 = cs(A_ROPE)
    one, zero = jnp.ones((r, 1), F32), jnp.zeros((r, 1), F32)
    z16 = jnp.zeros((r, A_ROPE // 2), F32)
    ca = jnp.concatenate([jnp.tile(one, (1, A_NOPE)), c16, c16, jnp.tile(one, (1, 32))], axis=1)
    saa = jnp.concatenate([jnp.tile(zero, (1, A_NOPE)), -s16, z16, jnp.tile(zero, (1, 32))], axis=1)
    sba = jnp.concatenate([jnp.tile(zero, (1, A_NOPE)), z16, s16, jnp.tile(zero, (1, 32))], axis=1)
    c32, s32 = cs(B_HEAD_DIM)
    z32 = jnp.zeros_like(s32)
    cb = jnp.concatenate([c32] * 4, axis=1)
    sab = jnp.concatenate([-s32, z32] * 2, axis=1)
    sbb = jnp.concatenate([z32, s32] * 2, axis=1)
    return ca, saa, sba, cb, sab, sbb


def _prep_proj_params(p):
    offs = np.cumsum((0,) + IN_SIZES)
    w_in = p['w_in']
    w = [w_in[:, offs[i]:offs[i + 1]] for i in range(len(IN_SIZES))]
    w_qa, w_ckv, w_kr, w_qb, w_kb, w_vb, w_qi, w_ki, w_wi, w_gate = w
    d = D_MODEL

    def blocks(wm, heads, dim):
        wm = wm.reshape(wm.shape[0], heads, dim)
        return jnp.pad(wm, ((0, 0), (0, 0), (0, HEAD_BLOCK - dim))).reshape(wm.shape[0], heads * HEAD_BLOCK)

    w_uq = blocks(p['w_uq'], A_HEADS, A_NOPE + A_ROPE)
    w_uk_c = p['w_uk'].reshape(A_KV_RANK, A_HEADS * A_NOPE)
    w_uk = blocks(w_uk_c, A_HEADS, A_NOPE)
    w_uv = p['w_uv'].reshape(A_KV_RANK, A_HEADS * A_V)
    w_kr = jnp.pad(w_kr, ((0, 0), (A_NOPE, LANES - A_NOPE - A_ROPE)))
    qb_cols, gqb = [], []
    zc = jnp.zeros((d, B_HEAD_DIM), F32)
    zg = jnp.zeros((B_HEAD_DIM,), F32)
    for h in range(B_HEADS):
        wh = w_qb[:, h * B_HEAD_DIM:(h + 1) * B_HEAD_DIM]
        qb_cols += [zc, wh] if _b_half(h) else [wh, zc]
        gqb += [zg, p['g_q_b']] if _b_half(h) else [p['g_q_b'], zg]
    w_qb = jnp.concatenate(qb_cols, axis=1)
    w_qi = blocks(w_qi, IDX_HEADS, IDX_DIM)
    w_ki = jnp.pad(w_ki, ((0, 0), (0, LANES - IDX_DIM)))
    w_wi = jnp.pad(w_wi, ((0, 0), (0, LANES - IDX_HEADS)))

    z32 = jnp.zeros((HEAD_BLOCK - A_NOPE - A_ROPE,), F32)
    gain_qa = jnp.tile(jnp.concatenate([p['g_qn_a'], p['g_qr_a'], z32]), A_HEADS) * (A_SCALE * LOG2E)
    gain_kn = jnp.tile(jnp.concatenate([p['g_kn_a'], jnp.zeros((HEAD_BLOCK - A_NOPE,), F32)]), A_HEADS)
    gain_kr = jnp.concatenate([jnp.zeros((A_NOPE,), F32), p['g_kr_a'], z32])
    gain_qb = jnp.concatenate(gqb) * (B_SCALE * LOG2E)
    gain_kb = jnp.tile(p['g_k_b'], B_KV_HEADS)

    qa_groups = []
    for h in range(A_HEADS):
        qa_groups += [(h * HEAD_BLOCK, A_NOPE), (h * HEAD_BLOCK + A_NOPE, A_ROPE)]
    kn_groups = [(h * HEAD_BLOCK, A_NOPE) for h in range(A_HEADS)]
    qb_groups = [(h * HEAD_BLOCK + B_HEAD_DIM * _b_half(h), B_HEAD_DIM) for h in range(B_HEADS)]
    kb_groups = [(g * B_HEAD_DIM, B_HEAD_DIM) for g in range(B_KV_HEADS)]

    row = lambda v: v.reshape(1, -1).astype(F32)
    b16 = lambda m: m.astype(BF16)
    proj = [row(p['g_attn']),
            b16(w_qa), row(p['g_q_a']), b16(w_uq), *_group_mats(A_HEADS * HEAD_BLOCK, qa_groups), row(gain_qa),
            b16(w_ckv), row(p['g_kv_a']), b16(w_kr), row(gain_kr),
            b16(w_uk), *_group_mats(A_HEADS * HEAD_BLOCK, kn_groups), row(gain_kn), b16(w_uv),
            b16(w_qb), *_group_mats(B_HEADS * HEAD_BLOCK, qb_groups), row(gain_qb),
            b16(w_kb), *_group_mats(B_KV_HEADS * B_HEAD_DIM, kb_groups), row(gain_kb),
            b16(w_vb), b16(w_qi), b16(w_ki), b16(w_wi)]
    extra = dict(w_gate=b16(w_gate), w_uk_blocks=b16(w_uk), w_uk_compact=b16(w_uk_c), w_uv=b16(w_uv),
                 g_kn=row(p['g_kn_a']))
    return proj, extra


def _proj_call(x, tables, proj_params):
    m = x.shape[0]
    t = min(PROJ_TILE, m)
    assert m % t == 0 and tables[0].shape[0] % t == 0
    nt = tables[0].shape[0] // t
    row_spec = lambda w: pl.BlockSpec((t, w), lambda i: (i, 0))
    tab_spec = pl.BlockSpec((t, LANES), lambda i: (i % nt, 0))
    hb = A_HEADS * HEAD_BLOCK
    outs = [(hb, BF16), (hb, BF16), (A_HEADS * A_V, BF16), (A_KV_RANK, F32), (A_ROPE, F32),
            (B_HEADS * HEAD_BLOCK, BF16), (B_KV_HEADS * B_HEAD_DIM, F32), (B_KV_HEADS * B_HEAD_DIM, BF16),
            (B_KV_HEADS * B_HEAD_DIM, F32), (B_KV_HEADS * B_HEAD_DIM, BF16),
            (IDX_HEADS * HEAD_BLOCK, BF16), (IDX_DIM, F32), (LANES, BF16), (LANES, F32)]
    return pl.pallas_call(
        _proj_kernel,
        grid=(m // t,),
        in_specs=[row_spec(D_MODEL), _const_spec(proj_params[0])] + [tab_spec] * 6
                 + [_const_spec(a) for a in proj_params[1:]],
        out_specs=[row_spec(w) for w, _ in outs],
        out_shape=[jax.ShapeDtypeStruct((m, w), dt) for w, dt in outs],
        compiler_params=pltpu.CompilerParams(dimension_semantics=("arbitrary",), vmem_limit_bytes=VMEM_LIMIT),
        name="proj",
    )(x, proj_params[0], *tables, *proj_params[1:])


def _mla_prompt_kernel(q_ref, k_ref, v_ref, o_ref, *, tq):
    i = pl.program_id(2)
    rows = lax.broadcasted_iota(I32, (tq, tq), 0)
    cols = lax.broadcasted_iota(I32, (tq, tq), 1)
    causal = cols <= rows
    lane = lax.broadcasted_iota(I32, (tq, LANES), 1)
    qs = [q_ref[:, hh * HEAD_BLOCK:(hh + 1) * HEAD_BLOCK] for hh in range(2)]

    def step(j, carry, masked):
        off = pl.multiple_of(j * tq, tq)
        v = v_ref[pl.ds(off, tq), :]
        new = []
        for hh in range(2):
            m, l, acc = carry[hh]
            s = _dot_nt(qs[hh], k_ref[pl.ds(off, tq), hh * HEAD_BLOCK:(hh + 1) * HEAD_BLOCK])
            if masked:
                s = jnp.where(causal, s, NEG)
            m_new = jnp.maximum(m, jnp.max(s, axis=-1, keepdims=True))
            p = jnp.exp2(s - m_new)
            corr = jnp.exp2(m - m_new)
            l = l * corr + jnp.sum(p, axis=-1, keepdims=True)
            new.append((m_new, l, acc * corr + _dot(p.astype(BF16), v)))
        return tuple(new)

    init = tuple((jnp.full((tq, 1), NEG, F32), jnp.zeros((tq, 1), F32), jnp.zeros((tq, LANES), F32))
                 for _ in range(2))
    carry = lax.fori_loop(0, i, functools.partial(step, masked=False), init)
    res = step(i, carry, masked=True)
    outs = [res[hh][2] / res[hh][1] for hh in range(2)]
    o_ref[...] = jnp.where(lane < A_V, outs[0], outs[1]).astype(o_ref.dtype)


def _mla_prompt_call(qa, kcat, va, n, s):
    tq = min(MLA_TQ, s)
    nq = s // tq
    pairs = A_HEADS // 2
    return pl.pallas_call(
        functools.partial(_mla_prompt_kernel, tq=tq),
        grid=(n, pairs, nq),
        in_specs=[pl.BlockSpec((tq, 2 * HEAD_BLOCK), lambda b, h, i: (b * nq + i, h)),
                  pl.BlockSpec((s, 2 * HEAD_BLOCK), lambda b, h, i: (b, h)),
                  pl.BlockSpec((s, 2 * A_V), lambda b, h, i: (b, h))],
        out_specs=pl.BlockSpec((tq, 2 * A_V), lambda b, h, i: (b * nq + i, h)),
        out_shape=jax.ShapeDtypeStruct((n * s, A_HEADS * A_V), BF16),
        compiler_params=pltpu.CompilerParams(dimension_semantics=("arbitrary",) * 3, vmem_limit_bytes=VMEM_LIMIT),
        name="mla_prompt",
    )(qa, kcat, va)


def _order_key(score):
    bits = pltpu.bitcast(score, I32)
    return bits ^ ((bits >> 31) & 0x7FFFFFFF)


def _select_threshold(count_ge, count_eq_below, shape, n_all, nsel, nbits, row_ok):
    c0 = count_ge(jnp.zeros(shape, I32))
    thr = jnp.where(c0 >= nsel, 0, INT_MIN).astype(I32)
    n_ge = jnp.where(c0 >= nsel, c0, n_all).astype(I32)

    def bit_body(p, carry):
        thr, n_ge = carry
        cand = thr + lax.shift_left(jnp.int32(1), 30 - p)
        cnt = count_ge(cand)
        ok = cnt >= nsel
        return jnp.where(ok, cand, thr), jnp.where(ok, cnt, n_ge)

    thr, n_ge = lax.fori_loop(0, 31, bit_body, (thr, n_ge))
    tie = (n_ge > nsel) & (thr > INT_MIN) & row_ok
    all_cols = jnp.full(shape, 2 ** 30, I32)

    def resolve():
        need = nsel - jnp.where(thr == 2 ** 31 - 1, 0, count_ge(thr + 1))

        def jbit(p, jmax):
            cand = jmax | lax.shift_left(jnp.int32(1), nbits - 1 - p)
            return jnp.where(count_eq_below(thr, cand) < need, cand, jmax)
        jmax = lax.fori_loop(0, nbits, jbit, jnp.zeros(shape, I32))
        return jnp.where(tie, jmax, all_cols)

    jmax = lax.cond(jnp.max(tie.astype(I32)) > 0, resolve, lambda: all_cols)
    return thr, jmax


def _dsa_prompt_kernel(qi_ref, wi_ref, qb_ref, ki_ref, kb_ref, vb_ref, o_ref, key_scr, *, tq, tk, nsel, nbits):
    i = pl.program_id(1)
    q0 = i * tq
    nch = q0 // tk + 1
    row_g = q0 + lax.broadcasted_iota(I32, (tq, tk), 0)
    col_l = lax.broadcasted_iota(I32, (tq, tk), 1)

    qi = jnp.concatenate([qi_ref[:, h * HEAD_BLOCK:(h + 1) * HEAD_BLOCK] for h in range(IDX_HEADS)], axis=0)
    w = wi_ref[...]
    wcol = [w[:, h:h + 1] for h in range(IDX_HEADS)]

    def idx_chunk(c, carry):
        off = pl.multiple_of(c * tk, tk)
        s = _dot_nt(qi, ki_ref[pl.ds(off, tk), :])
        score = jnp.zeros((tq, tk), F32)
        for h in range(IDX_HEADS):
            score = score + wcol[h] * jnp.maximum(s[h * tq:(h + 1) * tq], 0.0)
        key_scr[:, pl.ds(off, tk)] = jnp.where(off + col_l <= row_g, _order_key(score), INT_MIN)
        return carry

    lax.fori_loop(0, nch, idx_chunk, 0)

    rb = min(COUNT_ROWS, tq)
    lane_l = lax.broadcasted_iota(I32, (rb, LANES), 1)

    def count(pred):
        def body(c, accs):
            off = pl.multiple_of(c * tk, tk)
            new = []
            for b in range(tq // rb):
                sl = slice(b * rb, (b + 1) * rb)
                acc = accs[b]
                for jj in range(tk // LANES):
                    k = key_scr[sl, pl.ds(off + jj * LANES, LANES)]
                    acc = acc + jnp.where(pred(k, off + jj * LANES + lane_l, sl), 1, 0).astype(I32)
                new.append(acc)
            return tuple(new)
        accs = lax.fori_loop(0, nch, body, tuple(jnp.zeros((rb, LANES), I32) for _ in range(tq // rb)))
        tot = jnp.sum(jnp.concatenate(accs, axis=0), axis=-1, keepdims=True)
        return jnp.broadcast_to(tot, (tq, LANES))

    count_ge = lambda cand: count(lambda k, col, sl: k >= cand[sl])
    count_eq_below = lambda thr, j: count(lambda k, col, sl: (k == thr[sl]) & (col < j[sl]))
    thr, jmax = _select_threshold(count_ge, count_eq_below, (tq, LANES), nch * tk, nsel, nbits,
                                  lax.broadcasted_iota(I32, (tq, LANES), 0) >= 0)
    thr, jmax = thr[:, :1], jmax[:, :1]

    hp = B_HEADS // 2
    qs = [jnp.concatenate([qb_ref[:, h * HEAD_BLOCK:(h + 1) * HEAD_BLOCK] for h in range(p * hp, (p + 1) * hp)],
                          axis=0) for p in range(2)]

    def att_chunk(c, carry):
        off = pl.multiple_of(c * tk, tk)
        key = key_scr[:, pl.ds(off, tk)]
        col = off + col_l
        sel = ((key > thr) | ((key == thr) & (col <= jmax))) & (col <= row_g)
        new = []
        for p in range(2):
            m, l, acc = carry[p]
            kc = kb_ref[pl.ds(off, tk), p * LANES:(p + 1) * LANES]
            vc = vb_ref[pl.ds(off, tk), p * LANES:(p + 1) * LANES]
            s = jnp.where(sel[None], _dot_nt(qs[p], kc).reshape(hp, tq, tk), NEG)
            m_new = jnp.maximum(m, jnp.max(s, axis=-1, keepdims=True))
            pe = jnp.exp2(s - m_new)
            corr = jnp.exp2(m - m_new)
            l = l * corr + jnp.sum(pe, axis=-1, keepdims=True)
            pv = _dot(pe.reshape(hp * tq, tk).astype(BF16), vc).reshape(hp, tq, LANES)
            new.append((m_new, l, acc * corr + pv))
        return tuple(new)

    init = tuple((jnp.full((hp, tq, 1), NEG, F32), jnp.zeros((hp, tq, 1), F32), jnp.zeros((hp, tq, LANES), F32))
                 for _ in range(2))
    res = lax.fori_loop(0, nch, att_chunk, init)
    o_ref[...] = _assemble_b_heads([res[p][2] / res[p][1] for p in range(2)], tq).astype(o_ref.dtype)


def _assemble_b_heads(accs, rows):
    lane = lax.broadcasted_iota(I32, (rows, LANES), 1)
    swap = lambda a: pltpu.roll(a, B_HEAD_DIM, 1)
    blocks = []
    for p in range(2):
        a = accs[p]
        blocks.append(jnp.where(lane < B_HEAD_DIM, a[0], swap(a[1])))
        blocks.append(jnp.where(lane < B_HEAD_DIM, swap(a[2]), a[3]))
    return jnp.concatenate(blocks, axis=1)


def _dsa_prompt_call(qi, wi, qb, ki16, kb16, vb16, n, s):
    tq = min(DSA_TQ, s)
    tk = min(DSA_TK, s)
    nq = s // tq
    nsel = min(TOPK_MAX, s // 4)
    kvw = B_KV_HEADS * B_HEAD_DIM
    qspec = lambda w: pl.BlockSpec((tq, w), lambda b, i: (b * nq + i, 0))
    kspec = lambda w: pl.BlockSpec((s, w), lambda b, i: (b, 0))
    return pl.pallas_call(
        functools.partial(_dsa_prompt_kernel, tq=tq, tk=tk, nsel=nsel, nbits=int(s).bit_length()),
        grid=(n, nq),
        in_specs=[qspec(IDX_HEADS * HEAD_BLOCK), qspec(LANES), qspec(B_HEADS * HEAD_BLOCK),
                  kspec(LANES), kspec(kvw), kspec(kvw)],
        out_specs=qspec(B_HEADS * B_HEAD_DIM),
        out_shape=jax.ShapeDtypeStruct((n * s, B_HEADS * B_HEAD_DIM), BF16),
        scratch_shapes=[pltpu.VMEM((tq, s), I32)],
        compiler_params=pltpu.CompilerParams(dimension_semantics=("arbitrary",) * 2, vmem_limit_bytes=VMEM_LIMIT),
        name="dsa_prompt",
    )(qi, wi, qb, ki16, kb16, vb16)


def _mla_sample_kernel(pt_ref, q_ref, latn_ref, krn_ref, qmask_ref, wukb_ref, wukt_ref, wuv_ref,
                       hmask_ref, place_ref, lat_hbm, kr_hbm, o_ref, lhs_scr, qr_scr, m_scr, l_scr, acc_scr,
                       lat_buf, kr_buf, sem, *, pps, ppc, page, n_tok, nseq, nj):
    n = pl.program_id(0)
    j = pl.program_id(1)
    rows = A_HEADS * T_PAD
    knw = A_HEADS * A_NOPE

    def page_copies(seq, jj, slot):
        cps = []
        for k in range(pps):
            pid = pt_ref[seq, jj * pps + k]
            cps.append(pltpu.make_async_copy(lat_hbm.at[pid], lat_buf.at[slot, k], sem.at[0, slot]))
            cps.append(pltpu.make_async_copy(kr_hbm.at[pid], kr_buf.at[slot, k], sem.at[1, slot]))
        return cps

    step = n * nj + j
    slot = step & 1

    @pl.when(step == 0)
    def _():
        for cp in page_copies(n, j, slot):
            cp.start()

    @pl.when(step + 1 < nseq * nj)
    def _():
        wrap = j + 1 == nj
        for cp in page_copies(jnp.where(wrap, n + 1, n), jnp.where(wrap, 0, j + 1), 1 - slot):
            cp.start()

    for cp in page_copies(n, j, slot):
        cp.wait()

    @pl.when(j == 0)
    def _():
        q8 = q_ref[0].astype(F32)
        qpad = (jnp.concatenate([q8] * A_HEADS, axis=0) * qmask_ref[...]).astype(BF16)
        lhs_scr[:knw, :] = wukt_ref[...]
        lhs_scr[knw:, :] = _dot_nt(qpad, wukb_ref[...]).astype(BF16)
        qr = jnp.concatenate([q8[:, h * HEAD_BLOCK:(h + 1) * HEAD_BLOCK] for h in range(A_HEADS)], axis=0)
        qr_scr[...] = _dot(qr.astype(BF16), place_ref[...]).astype(BF16)
        m_scr[...] = jnp.full((rows, 1), NEG, F32)
        l_scr[...] = jnp.zeros((rows, 1), F32)
        acc_scr[...] = jnp.zeros((rows, A_KV_RANK), F32)

    def partial(c, krt, mask):
        nk = c.shape[0]
        c16 = c.astype(BF16)
        both = _dot_nt(lhs_scr[...], c16)
        knt = both[:knw]
        msq = jnp.sum((knt * knt).reshape(A_HEADS, A_NOPE, nk), axis=1) * (1.0 / A_NOPE)
        r = lax.rsqrt(msq + EPS)
        r = jnp.broadcast_to(r[:, None, :], (A_HEADS, T_PAD, nk)).reshape(rows, nk)
        s = both[knw:] * r + _dot(qr_scr[...], krt.astype(BF16))
        if mask is not None:
            s = jnp.where(mask, s, NEG)
        mg = jnp.max(s, axis=-1, keepdims=True)
        p = jnp.exp2(s - mg)
        return mg, jnp.sum(p, axis=-1, keepdims=True), _dot(p.astype(BF16), c16)

    def merge(parts):
        m = m_scr[...]
        m_new = m
        for mg, _, _ in parts:
            m_new = jnp.maximum(m_new, mg)
        corr = jnp.exp2(m - m_new)
        l = l_scr[...] * corr
        acc = acc_scr[...] * corr
        for mg, lg, pv in parts:
            w = jnp.exp2(mg - m_new)
            l = l + lg * w
            acc = acc + pv * w
        m_scr[...] = m_new
        l_scr[...] = l
        acc_scr[...] = acc

    parts = []
    for g in range(pps // ppc):
        c = jnp.concatenate([lat_buf[slot, g * ppc + k] for k in range(ppc)], axis=0)
        krt = jnp.concatenate([kr_buf[slot, g * ppc + k] for k in range(ppc)], axis=1)
        parts.append(partial(c, krt, None))
    merge(parts)

    @pl.when(j == pl.num_programs(1) - 1)
    def _():
        r = lax.broadcasted_iota(I32, (rows, page), 0) & (T_PAD - 1)
        col = lax.broadcasted_iota(I32, (rows, page), 1)
        merge([partial(latn_ref[0], krn_ref[0], (col <= r) & (col < n_tok))])
        lat_out = (acc_scr[...] / l_scr[...]).astype(BF16)
        full = _dot(lat_out, wuv_ref[...])
        hm = hmask_ref[...]
        out = jnp.zeros((T_PAD, A_HEADS * A_V), F32)
        for h in range(A_HEADS):
            out = out + full[h * T_PAD:(h + 1) * T_PAD] * hm[h:h + 1]
        o_ref[0] = out.astype(o_ref.dtype)


def _pad_tokens(a, nseq, t, rows=T_PAD):
    a = a.reshape(nseq, t, a.shape[-1])
    return jnp.pad(a, ((0, 0), (0, rows - t), (0, 0)))


def _new_page_t(a, nseq, t, page):
    return jnp.swapaxes(_pad_tokens(a, nseq, t, page), 1, 2)


def _pages_t(cache):
    pool, page = cache.shape[:2]
    return jnp.moveaxis(cache, 1, -1).reshape(pool, -1, page)


def _mla_sample_call(page_table, qa, lat_new, kr_new, cache_lat, cache_kr_t, extra, t):
    nseq, npages = page_table.shape
    page = cache_lat.shape[1]
    pps = min(PAGES_PER_STEP, npages)
    ppc = min(PAGES_PER_CHUNK, pps)
    assert npages % pps == 0 and pps % ppc == 0
    rows = A_HEADS * T_PAD
    hb = A_HEADS * HEAD_BLOCK

    qmask = np.zeros((rows, hb), np.float32)
    hmask = np.zeros((A_HEADS, A_HEADS * A_V), np.float32)
    for h in range(A_HEADS):
        qmask[h * T_PAD:(h + 1) * T_PAD, h * HEAD_BLOCK:h * HEAD_BLOCK + A_NOPE] = 1.0
        hmask[h, h * A_V:(h + 1) * A_V] = 1.0
    gkn_blocks = jnp.tile(jnp.pad(extra['g_kn'], ((0, 0), (0, HEAD_BLOCK - A_NOPE))), (1, A_HEADS))
    qmask = jnp.asarray(qmask) * gkn_blocks
    place = np.zeros((HEAD_BLOCK, A_ROPE), np.float32)
    place[A_NOPE:A_NOPE + A_ROPE, :] = np.eye(A_ROPE, dtype=np.float32)
    consts = [qmask, extra['w_uk_blocks'], extra['w_uk_compact'].T, extra['w_uv'],
              jnp.asarray(hmask), jnp.asarray(place, BF16)]

    seq_spec = lambda r, w: pl.BlockSpec((1, r, w), lambda n, j, pt: (n, 0, 0))
    cspec = lambda a: pl.BlockSpec(a.shape, lambda n, j, pt: (0,) * a.ndim)
    hbm_spec = pl.BlockSpec(memory_space=pl.ANY)
    nj = npages // pps
    grid_spec = pltpu.PrefetchScalarGridSpec(
        num_scalar_prefetch=1,
        grid=(nseq, nj),
        in_specs=[seq_spec(T_PAD, hb), seq_spec(page, A_KV_RANK), seq_spec(A_ROPE, page)] + [cspec(a) for a in consts]
                 + [hbm_spec, hbm_spec],
        out_specs=seq_spec(T_PAD, A_HEADS * A_V),
        scratch_shapes=[pltpu.VMEM((A_HEADS * A_NOPE + rows, A_KV_RANK), BF16), pltpu.VMEM((rows, A_ROPE), BF16),
                        pltpu.VMEM((rows, 1), F32), pltpu.VMEM((rows, 1), F32), pltpu.VMEM((rows, A_KV_RANK), F32),
                        pltpu.VMEM((2, pps, page, A_KV_RANK), F32), pltpu.VMEM((2, pps, A_ROPE, page), F32),
                        pltpu.SemaphoreType.DMA((2, 2))],
    )
    out = pl.pallas_call(
        functools.partial(_mla_sample_kernel, pps=pps, ppc=ppc, page=page, n_tok=t, nseq=nseq, nj=nj),
        grid_spec=grid_spec,
        out_shape=jax.ShapeDtypeStruct((nseq, T_PAD, A_HEADS * A_V), BF16),
        compiler_params=pltpu.CompilerParams(dimension_semantics=("arbitrary",) * 2, vmem_limit_bytes=VMEM_LIMIT),
        name="mla_sample",
    )(page_table, _pad_tokens(qa, nseq, t), _pad_tokens(lat_new, nseq, t, page), _new_page_t(kr_new, nseq, t, page),
      *consts, cache_lat, cache_kr_t)
    return out[:, :t].reshape(nseq * t, A_HEADS * A_V)


def _dsa_sample_kernel(pt_ref, qi_ref, wi_ref, qb_ref, kin_ref, kbn_ref, vbn_ref, ik_hbm, k_hbm, v_hbm,
                       o_ref, key_scr, thr_scr, jmax_scr, m_scr, l_scr, acc_scr, ik_buf, k_buf, v_buf, sem,
                       *, pps, page, nsel, nbits, n_tok, past, ngroups, gsz, nj):
    g = pl.program_id(0)
    phase = pl.program_id(1)
    si = pl.program_id(2)
    j = pl.program_id(3)
    n = g * gsz + si
    last = nj - 1
    kt = pps * page
    hp = B_HEADS // 2
    tok = lax.broadcasted_iota(I32, (T_PAD, 1), 0)
    row0 = pl.multiple_of(si * T_PAD, T_PAD)

    def page_copies(seq, ph, jj, slot):
        cps = []
        for k in range(pps):
            pid = pt_ref[seq, jj * pps + k]
            if ph == 0:
                cps.append(pltpu.make_async_copy(ik_hbm.at[pid], ik_buf.at[slot, k], sem.at[0, slot]))
            else:
                cps.append(pltpu.make_async_copy(k_hbm.at[pid], k_buf.at[slot, k], sem.at[1, slot]))
                cps.append(pltpu.make_async_copy(v_hbm.at[pid], v_buf.at[slot, k], sem.at[2, slot]))
        return cps

    def for_phase(ph_value, fn):
        for ph in range(2):
            pl.when(ph_value == ph)(functools.partial(fn, ph))

    step = ((g * 2 + phase) * gsz + si) * nj + j
    slot = step & 1

    def start_all(ph, seq, jj, slot):
        for cp in page_copies(seq, ph, jj, slot):
            cp.start()

    @pl.when(step == 0)
    def _():
        start_all(0, n, j, slot)

    @pl.when(step + 1 < ngroups * 2 * gsz * nj)
    def _():
        wrap_j = j + 1 == nj
        wrap_s = wrap_j & (si + 1 == gsz)
        next_si = jnp.where(wrap_j, jnp.where(wrap_s, 0, si + 1), si)
        next_ph = jnp.where(wrap_s, 1 - phase, phase)
        next_g = jnp.where(wrap_s & (phase == 1), g + 1, g)
        for_phase(next_ph, functools.partial(start_all, seq=next_g * gsz + next_si,
                                             jj=jnp.where(wrap_j, 0, j + 1), slot=1 - slot))

    def wait_all(ph):
        for cp in page_copies(n, ph, j, slot):
            cp.wait()

    for_phase(phase, wait_all)

    def new_visible(width):
        col = lax.broadcasted_iota(I32, (T_PAD, width), 1)
        return (col <= tok) & (col < n_tok)

    @pl.when(phase == 0)
    def _():
        q8 = qi_ref[0].astype(F32)
        qi = jnp.concatenate([q8[:, h * HEAD_BLOCK:h * HEAD_BLOCK + IDX_DIM] for h in range(IDX_HEADS)],
                             axis=0).astype(BF16)
        w = wi_ref[0]

        def score(ikt16):
            s = _dot(qi, ikt16)
            tot = jnp.zeros((T_PAD, s.shape[1]), F32)
            for h in range(IDX_HEADS):
                tot = tot + w[:, h:h + 1] * jnp.maximum(s[h * T_PAD:(h + 1) * T_PAD], 0.0)
            return _order_key(tot)

        ikt = jnp.concatenate([ik_buf[slot, k] for k in range(pps)], axis=1).astype(BF16)
        key_scr[pl.ds(row0, T_PAD), pl.ds(pl.multiple_of(j * kt, kt), kt)] = score(ikt)

        @pl.when(j == last)
        def _():
            knew = score(kin_ref[0])
            key_scr[pl.ds(row0, T_PAD), past:past + page] = jnp.where(new_visible(page), knew, INT_MIN)

        @pl.when((j == last) & (si == gsz - 1))
        def _():
            grows = gsz * T_PAD
            col = lax.broadcasted_iota(I32, (grows, past + page), 1)
            row_tok = lax.broadcasted_iota(I32, (grows, 1), 0) & (T_PAD - 1)

            def count_ge(cand):
                return jnp.sum(jnp.where(key_scr[...] >= cand, 1, 0).astype(I32), axis=-1, keepdims=True)

            def count_eq_below(thr, jj):
                hit = (key_scr[...] == thr) & (col < jj)
                return jnp.sum(jnp.where(hit, 1, 0).astype(I32), axis=-1, keepdims=True)

            thr, jmax = _select_threshold(count_ge, count_eq_below, (grows, 1), past + page, nsel, nbits,
                                          row_tok < n_tok)
            thr_scr[...] = thr
            jmax_scr[...] = jmax

    @pl.when(phase == 1)
    def _():
        @pl.when(j == 0)
        def _():
            m_scr[...] = jnp.full(m_scr.shape, NEG, F32)
            l_scr[...] = jnp.zeros(l_scr.shape, F32)
            acc_scr[...] = jnp.zeros(acc_scr.shape, F32)

        q8 = qb_ref[0].astype(F32)
        qs = [jnp.concatenate([q8[:, h * HEAD_BLOCK:(h + 1) * HEAD_BLOCK] for h in range(p * hp, (p + 1) * hp)],
                              axis=0).astype(BF16) for p in range(2)]
        thr = thr_scr[pl.ds(row0, T_PAD), :]
        jmax = jmax_scr[pl.ds(row0, T_PAD), :]

        def attend(kt16, vt16, key, col0, extra_mask):
            width = key.shape[1]
            col = col0 + lax.broadcasted_iota(I32, (T_PAD, width), 1)
            sel = (key > thr) | ((key == thr) & (col <= jmax))
            if extra_mask is not None:
                sel = sel & extra_mask
            for p in range(2):
                s = _dot(qs[p], kt16[p * LANES:(p + 1) * LANES]).reshape(hp, T_PAD, width)
                s = jnp.where(sel[None], s, NEG)
                m = m_scr[p]
                m_new = jnp.maximum(m, jnp.max(s, axis=-1, keepdims=True))
                pe = jnp.exp2(s - m_new)
                corr = jnp.exp2(m - m_new)
                l_scr[p] = l_scr[p] * corr + jnp.sum(pe, axis=-1, keepdims=True)
                pv = _dot_nt(pe.reshape(hp * T_PAD, width).astype(BF16), vt16[p * LANES:(p + 1) * LANES])
                acc_scr[p] = acc_scr[p] * corr + pv.reshape(hp, T_PAD, LANES)
                m_scr[p] = m_new

        kt16 = jnp.concatenate([k_buf[slot, k] for k in range(pps)], axis=1).astype(BF16)
        vt16 = jnp.concatenate([v_buf[slot, k] for k in range(pps)], axis=1).astype(BF16)
        off = pl.multiple_of(j * kt, kt)
        attend(kt16, vt16, key_scr[pl.ds(row0, T_PAD), pl.ds(off, kt)], off, None)

        @pl.when(j == last)
        def _():
            attend(kbn_ref[0], vbn_ref[0], key_scr[pl.ds(row0, T_PAD), past:past + page], past, new_visible(page))
            o_ref[0] = _assemble_b_heads([acc_scr[p] / l_scr[p] for p in range(2)], T_PAD).astype(o_ref.dtype)


def _dsa_sample_call(page_table, qi, wi, qb, ki16, kb16, vb16, cache_ik_t, cache_k_t, cache_v_t, t):
    nseq, npages = page_table.shape
    page = cache_ik_t.shape[2]
    past = npages * page
    pps = min(PAGES_PER_STEP, npages)
    assert npages % pps == 0 and t <= T_PAD
    nj = npages // pps
    nsel = min(TOPK_MAX, (past + t) // 4)
    kvw = B_KV_HEADS * B_HEAD_DIM
    hp = B_HEADS // 2

    new_page = lambda a: _new_page_t(a, nseq, t, page)
    gsz = math.gcd(SELECT_GROUP, nseq)
    ngroups = nseq // gsz

    seq_spec = lambda r, w: pl.BlockSpec((1, r, w), lambda g, ph, si, j, pt: (g * gsz + si, 0, 0))
    out_spec = pl.BlockSpec((1, T_PAD, B_HEADS * B_HEAD_DIM), lambda g, ph, si, j, pt: (g * gsz + si * ph, 0, 0))
    hbm_spec = pl.BlockSpec(memory_space=pl.ANY)
    grid_spec = pltpu.PrefetchScalarGridSpec(
        num_scalar_prefetch=1,
        grid=(ngroups, 2, gsz, nj),
        in_specs=[seq_spec(T_PAD, IDX_HEADS * HEAD_BLOCK), seq_spec(T_PAD, LANES), seq_spec(T_PAD, B_HEADS * HEAD_BLOCK),
                  seq_spec(IDX_DIM, page), seq_spec(kvw, page), seq_spec(kvw, page)] + [hbm_spec] * 3,
        out_specs=out_spec,
        scratch_shapes=[pltpu.VMEM((gsz * T_PAD, past + page), I32), pltpu.VMEM((gsz * T_PAD, 1), I32),
                        pltpu.VMEM((gsz * T_PAD, 1), I32),
                        pltpu.VMEM((2, hp, T_PAD, 1), F32), pltpu.VMEM((2, hp, T_PAD, 1), F32),
                        pltpu.VMEM((2, hp, T_PAD, LANES), F32),
                        pltpu.VMEM((2, pps, IDX_DIM, page), F32), pltpu.VMEM((2, pps, kvw, page), F32),
                        pltpu.VMEM((2, pps, kvw, page), F32), pltpu.SemaphoreType.DMA((3, 2))],
    )
    out = pl.pallas_call(
        functools.partial(_dsa_sample_kernel, pps=pps, page=page, nsel=nsel, nbits=int(past + page).bit_length(),
                          n_tok=t, past=past, ngroups=ngroups, gsz=gsz, nj=nj),
        grid_spec=grid_spec,
        out_shape=jax.ShapeDtypeStruct((nseq, T_PAD, B_HEADS * B_HEAD_DIM), BF16),
        compiler_params=pltpu.CompilerParams(dimension_semantics=("arbitrary",) * 4, vmem_limit_bytes=VMEM_LIMIT),
        name="dsa_sample",
    )(page_table, _pad_tokens(qi, nseq, t), _pad_tokens(wi, nseq, t), _pad_tokens(qb, nseq, t),
      new_page(ki16[:, :IDX_DIM]), new_page(kb16), new_page(vb16),
      cache_ik_t, cache_k_t, cache_v_t)
    return out[:, :t].reshape(nseq * t, B_HEADS * B_HEAD_DIM)


def _merge_kernel(x_ref, oa_ref, ob_ref, gattn_ref, wgate_ref, woa_ref, wob_ref, wout_ref, h_ref):
    x = x_ref[...]
    n = _rms_full(x, gattn_ref[...]).astype(BF16)
    gate = jax.nn.sigmoid(_dot(n, wgate_ref[...]))
    mix = gate[:, :D_MODEL] * _dot(oa_ref[...], woa_ref[...]) + gate[:, D_MODEL:] * _dot(ob_ref[...], wob_ref[...])
    h_ref[...] = x + _dot(mix.astype(BF16), wout_ref[...])


def _mlp_kernel(h_ref, gmlp_ref, wup_ref, wdown_ref, y_ref):
    h = h_ref[...]
    u = jnp.maximum(_dot(_rms_full(h, gmlp_ref[...]).astype(BF16), wup_ref[...]), 0.0)
    y_ref[...] = h + _dot((u * u).astype(BF16), wdown_ref[...])


def _output_call(x, o_a, o_b, g_attn, w_gate, p):
    m = x.shape[0]
    t = min(PROJ_TILE, m)
    row_spec = lambda w: pl.BlockSpec((t, w), lambda i: (i, 0))
    params = pltpu.CompilerParams(dimension_semantics=("arbitrary",), vmem_limit_bytes=VMEM_LIMIT)
    consts = [g_attn, w_gate, p['w_o_a'].astype(BF16), p['w_o_b'].astype(BF16), p['w_out'].astype(BF16)]
    h = pl.pallas_call(
        _merge_kernel,
        grid=(m // t,),
        in_specs=[row_spec(D_MODEL), row_spec(o_a.shape[1]), row_spec(o_b.shape[1])] + [_const_spec(a) for a in consts],
        out_specs=row_spec(D_MODEL),
        out_shape=jax.ShapeDtypeStruct((m, D_MODEL), F32),
        compiler_params=params,
        name="merge",
    )(x, o_a, o_b, *consts)
    consts = [p['g_mlp'].reshape(1, -1), p['w_up'].astype(BF16), p['w_down'].astype(BF16)]
    return pl.pallas_call(
        _mlp_kernel,
        grid=(m // t,),
        in_specs=[row_spec(D_MODEL)] + [_const_spec(a) for a in consts],
        out_specs=row_spec(D_MODEL),
        out_shape=jax.ShapeDtypeStruct((m, D_MODEL), F32),
        compiler_params=params,
        name="mlp",
    )(h, *consts)


def _layer(h_p, h_s, caches, page_table, p):
    n, s, _ = h_p.shape
    nseq, t, _ = h_s.shape
    past = page_table.shape[1] * caches['lat'].shape[1]
    proj_params, extra = _prep_proj_params(p)
    g_attn = proj_params[0]

    xs = h_s.reshape(nseq * t, D_MODEL)
    pos_s = jnp.tile(past + jnp.arange(t), nseq)
    (qa, kcat, va, lat, krope, qb, kb32, kb16, vb32, vb16, qi, ki32, ki16, wi) = _proj_call(
        xs, _rope_tables(pos_s), proj_params)
    o_a = _mla_sample_call(page_table, qa, lat, krope, caches['lat'], _pages_t(caches['kr']), extra, t)
    o_b = _dsa_sample_call(page_table, qi, wi, qb, ki16, kb16, vb16, _pages_t(caches['ik']),
                           _pages_t(caches['k']), _pages_t(caches['v']), t)
    y_s = _output_call(xs, o_a, o_b, g_attn, extra['w_gate'], p).reshape(nseq, t, D_MODEL)
    new_s = (lat.reshape(nseq, t, -1), krope.reshape(nseq, t, -1), kb32.reshape(nseq, t, B_KV_HEADS, B_HEAD_DIM),
             vb32.reshape(nseq, t, B_KV_HEADS, B_HEAD_DIM), ki32.reshape(nseq, t, -1))

    xp = h_p.reshape(n * s, D_MODEL)
    (qa, kcat, va, lat, krope, qb, kb32, kb16, vb32, vb16, qi, ki32, ki16, wi) = _proj_call(
        xp, _rope_tables(jnp.arange(s)), proj_params)
    o_a = _mla_prompt_call(qa, kcat, va, n, s)
    o_b = _dsa_prompt_call(qi, wi, qb, ki16, kb16, vb16, n, s)
    y_p = _output_call(xp, o_a, o_b, g_attn, extra['w_gate'], p).reshape(n, s, D_MODEL)
    new_p = (lat.reshape(n, s, -1), krope.reshape(n, s, -1), kb32.reshape(n, s, B_KV_HEADS, B_HEAD_DIM),
             vb32.reshape(n, s, B_KV_HEADS, B_HEAD_DIM), ki32.reshape(n, s, -1))
    return y_p, y_s, new_p, new_s


def kernel(x_prompt, x_sample, cache_mla_latent, cache_mla_krope, cache_dsa_k, cache_dsa_v, cache_idx_k, page_table, g_attn, w_in, g_q_a, w_uq, g_qn_a, g_qr_a, g_kv_a, g_kr_a, w_uk, g_kn_a, w_uv, w_o_a, g_q_b, g_k_b, w_o_b, w_out, g_mlp, w_up, w_down):
    depth = w_in.shape[0]
    h_p, h_s = x_prompt, x_sample
    new_p, new_s = [], []
    for l in range(depth):
        p = dict(g_attn=g_attn[l], w_in=w_in[l], g_q_a=g_q_a[l], w_uq=w_uq[l], g_qn_a=g_qn_a[l],
                 g_qr_a=g_qr_a[l], g_kv_a=g_kv_a[l], g_kr_a=g_kr_a[l], w_uk=w_uk[l], g_kn_a=g_kn_a[l],
                 w_uv=w_uv[l], w_o_a=w_o_a[l], g_q_b=g_q_b[l], g_k_b=g_k_b[l], w_o_b=w_o_b[l],
                 w_out=w_out[l], g_mlp=g_mlp[l], w_up=w_up[l], w_down=w_down[l])
        caches = dict(lat=cache_mla_latent[l], kr=cache_mla_krope[l], k=cache_dsa_k[l], v=cache_dsa_v[l],
                      ik=cache_idx_k[l])
        h_p, h_s, np_l, ns_l = _layer(h_p, h_s, caches, page_table, p)
        new_p.append(np_l)
        new_s.append(ns_l)
    stack = lambda parts, i: jnp.stack([q[i] for q in parts])
    return (h_p, h_s, *[stack(new_p, i) for i in range(5)], *[stack(new_s, i) for i in range(5)])
```

```python
import functools
import math

import jax
import jax.numpy as jnp
import numpy as np
from jax import lax
from jax.experimental import pallas as pl
from jax.experimental.pallas import tpu as pltpu

F32 = jnp.float32
BF16 = jnp.bfloat16
I32 = jnp.int32

D_MODEL = 1024
A_HEADS = 8
A_NOPE = 64
A_ROPE = 32
A_V = 64
A_Q_RANK = 768
A_KV_RANK = 256
A_SCALE = 1.0 / math.sqrt(A_NOPE + A_ROPE)
B_HEADS = 8
B_KV_HEADS = 4
B_HEAD_DIM = 64
B_SCALE = 1.0 / math.sqrt(B_HEAD_DIM)
IDX_HEADS = 8
IDX_DIM = 64
IDX_SCALE = 1.0 / math.sqrt(IDX_DIM)
TOPK_MAX = 256
D_FF = 4 * D_MODEL
ROPE_THETA = 10000.0
EPS = 1e-6
IN_SIZES = (A_Q_RANK, A_KV_RANK, A_ROPE, B_HEADS * B_HEAD_DIM, B_KV_HEADS * B_HEAD_DIM,
            B_KV_HEADS * B_HEAD_DIM, IDX_HEADS * IDX_DIM, IDX_DIM, IDX_HEADS, 2 * D_MODEL)

LANES = 128
HEAD_BLOCK = LANES
NEG = -1e30
INT_MIN = -(2 ** 31)
VMEM_LIMIT = 56 * 1024 * 1024
DMA_PRIORITIES = 2

LOG2E = math.log2(math.e)

PROJ_TILE = 256
MLA_TQ = 1024
DSA_TQ = 256
DSA_TK = 1024
COUNT_ROWS = 64
PAGES_PER_STEP = 32
PAGES_PER_CHUNK = 8
T_PAD = 8
SELECT_GROUP = 8


def _dot(a, b):
    return jnp.dot(a, b, preferred_element_type=F32)


def _dot_nt(a, b):
    return lax.dot_general(a, b, (((1,), (1,)), ((), ())), preferred_element_type=F32)


def _rms_full(x, g):
    ms = jnp.mean(x * x, axis=-1, keepdims=True)
    return x * lax.rsqrt(ms + EPS) * g


def _split_bf16(x):
    hi = x.astype(BF16)
    lo = (x - hi.astype(F32)).astype(BF16)
    return hi, lo


def _group_rms(x, gs_ref, inv_ref, e_ref):
    hi, lo = _split_bf16(x * x)
    gs = gs_ref[...]
    ssum = _dot(hi, gs) + _dot(lo, gs)
    r = lax.rsqrt(ssum * inv_ref[...] + EPS)
    rhi, rlo = _split_bf16(r)
    e = e_ref[...]
    return x * (_dot(rhi, e) + _dot(rlo, e))


def _rope(x, tabs, half):
    c, sa, sb = tabs
    outs = []
    for j in range(x.shape[1] // LANES):
        xc = x[:, j * LANES:(j + 1) * LANES]
        outs.append(xc * c + pltpu.roll(xc, LANES - half, 1) * sa + pltpu.roll(xc, half, 1) * sb)
    return outs[0] if len(outs) == 1 else jnp.concatenate(outs, axis=1)


def _proj_kernel(x_ref, gattn_ref, ca_ref, saa_ref, sba_ref, cb_ref, sab_ref, sbb_ref,
                 wqa_ref, gqa_ref, wuq_ref, gsqa_ref, invqa_ref, eqa_ref, gainqa_ref,
                 wckv_ref, gkva_ref, wkr_ref, gainkr_ref,
                 wuk_ref, gskn_ref, invkn_ref, ekn_ref, gainkn_ref, wuv_ref,
                 wqb_ref, gsqb_ref, invqb_ref, eqb_ref, gainqb_ref,
                 wkb_ref, gskb_ref, invkb_ref, ekb_ref, gainkb_ref,
                 wvb_ref, wqi_ref, wki_ref, wwi_ref,
                 qa_o, kcat_o, va_o, lat_o, krope_o, qb_o, kb32_o, kb16_o, vb32_o, vb16_o,
                 qi_o, ki32_o, ki16_o, wi_o):
    n = _rms_full(x_ref[...], gattn_ref[...]).astype(BF16)
    tab_a = (ca_ref[...], saa_ref[...], sba_ref[...])
    tab_b = (cb_ref[...], sab_ref[...], sbb_ref[...])

    qlat = _rms_full(_dot(n, wqa_ref[...]), gqa_ref[...]).astype(BF16)
    qa = _group_rms(_dot(qlat, wuq_ref[...]), gsqa_ref, invqa_ref, eqa_ref) * gainqa_ref[...]
    qa_o[...] = _rope(qa, tab_a, A_ROPE // 2).astype(BF16)

    c = _rms_full(_dot(n, wckv_ref[...]), gkva_ref[...])
    lat_o[...] = c
    c16 = c.astype(BF16)
    kr = _dot(n, wkr_ref[...])
    ms = jnp.sum(kr * kr, axis=-1, keepdims=True) * (1.0 / A_ROPE)
    kr = _rope(kr * lax.rsqrt(ms + EPS) * gainkr_ref[...], tab_a, A_ROPE // 2)
    krope_o[...] = kr[:, A_NOPE:A_NOPE + A_ROPE]
    kn = _group_rms(_dot(c16, wuk_ref[...]), gskn_ref, invkn_ref, ekn_ref) * gainkn_ref[...]
    kcat_o[...] = (kn + jnp.concatenate([kr] * A_HEADS, axis=1)).astype(BF16)
    va_o[...] = _dot(c16, wuv_ref[...]).astype(BF16)

    qb = _group_rms(_dot(n, wqb_ref[...]), gsqb_ref, invqb_ref, eqb_ref) * gainqb_ref[...]
    qb_o[...] = _rope(qb, tab_b, B_HEAD_DIM // 2).astype(BF16)
    kb = _group_rms(_dot(n, wkb_ref[...]), gskb_ref, invkb_ref, ekb_ref) * gainkb_ref[...]
    kb = _rope(kb, tab_b, B_HEAD_DIM // 2)
    kb32_o[...] = kb
    kb16_o[...] = kb.astype(BF16)
    vb = _dot(n, wvb_ref[...])
    vb32_o[...] = vb
    vb16_o[...] = vb.astype(BF16)

    qi = _rope(_dot(n, wqi_ref[...]), tab_b, IDX_DIM // 2) * IDX_SCALE
    qi_o[...] = qi.astype(BF16)
    ki = _rope(_dot(n, wki_ref[...]), tab_b, IDX_DIM // 2)
    ki32_o[...] = ki[:, :IDX_DIM]
    ki16_o[...] = ki.astype(BF16)
    wi_o[...] = _dot(n, wwi_ref[...]) * (IDX_HEADS ** -0.5)


def _const_spec(a):
    nd = a.ndim
    return pl.BlockSpec(a.shape, lambda *_: (0,) * nd)


def _group_mats(width, groups):
    gs = np.zeros((width, LANES), np.float32)
    inv = np.ones((1, LANES), np.float32)
    for j, (start, size) in enumerate(groups):
        gs[start:start + size, j] = 1.0
        inv[0, j] = 1.0 / size
    return jnp.asarray(gs, BF16), jnp.asarray(inv), jnp.asarray(gs.T, BF16)


def _b_half(h):
    return (h // (B_HEADS // B_KV_HEADS)) % 2


def _rope_tables(pos):
    posf = pos.astype(F32)[:, None]

    def cs(d):
        inv = ROPE_THETA ** (-jnp.arange(0, d, 2, dtype=F32) / d)
        ang = posf * inv[None, :]
        return jnp.cos(ang), jnp.sin(ang)

    r = pos.shape[0]
    c16, s16 = cs(A_ROPE)
    one, zero = jnp.ones((r, 1), F32), jnp.zeros((r, 1), F32)
    z16 = jnp.zeros((r, A_ROPE // 2), F32)
    ca = jnp.concatenate([jnp.tile(one, (1, A_NOPE)), c16, c16, jnp.tile(one, (1, 32))], axis=1)
    saa = jnp.concatenate([jnp.tile(zero, (1, A_NOPE)), -s16, z16, jnp.tile(zero, (1, 32))], axis=1)
    sba = jnp.concatenate([jnp.tile(zero, (1, A_NOPE)), z16, s16, jnp.tile(zero, (1, 32))], axis=1)
    c32, s32 = cs(B_HEAD_DIM)
    z32 = jnp.zeros_like(s32)
    cb = jnp.concatenate([c32] * 4, axis=1)
    sab = jnp.concatenate([-s32, z32] * 2, axis=1)
    sbb = jnp.concatenate([z32, s32] * 2, axis=1)
    return ca, saa, sba, cb, sab, sbb


def _prep_proj_params(p):
    offs = np.cumsum((0,) + IN_SIZES)
    w_in = p['w_in']
    w = [w_in[:, offs[i]:offs[i + 1]] for i in range(len(IN_SIZES))]
    w_qa, w_ckv, w_kr, w_qb, w_kb, w_vb, w_qi, w_ki, w_wi, w_gate = w
    d = D_MODEL

    def blocks(wm, heads, dim):
        wm = wm.reshape(wm.shape[0], heads, dim)
        return jnp.pad(wm, ((0, 0), (0, 0), (0, HEAD_BLOCK - dim))).reshape(wm.shape[0], heads * HEAD_BLOCK)

    w_uq = blocks(p['w_uq'], A_HEADS, A_NOPE + A_ROPE)
    w_uk_c = p['w_uk'].reshape(A_KV_RANK, A_HEADS * A_NOPE)
    w_uk = blocks(w_uk_c, A_HEADS, A_NOPE)
    w_uv = p['w_uv'].reshape(A_KV_RANK, A_HEADS * A_V)
    w_kr = jnp.pad(w_kr, ((0, 0), (A_NOPE, LANES - A_NOPE - A_ROPE)))
    qb_cols, gqb = [], []
    zc = jnp.zeros((d, B_HEAD_DIM), F32)
    zg = jnp.zeros((B_HEAD_DIM,), F32)
    for h in range(B_HEADS):
        wh = w_qb[:, h * B_HEAD_DIM:(h + 1) * B_HEAD_DIM]
        qb_cols += [zc, wh] if _b_half(h) else [wh, zc]
        gqb += [zg, p['g_q_b']] if _b_half(h) else [p['g_q_b'], zg]
    w_qb = jnp.concatenate(qb_cols, axis=1)
    w_qi = blocks(w_qi, IDX_HEADS, IDX_DIM)
    w_ki = jnp.pad(w_ki, ((0, 0), (0, LANES - IDX_DIM)))
    w_wi = jnp.pad(w_wi, ((0, 0), (0, LANES - IDX_HEADS)))

    z32 = jnp.zeros((HEAD_BLOCK - A_NOPE - A_ROPE,), F32)
    gain_qa = jnp.tile(jnp.concatenate([p['g_qn_a'], p['g_qr_a'], z32]), A_HEADS) * (A_SCALE * LOG2E)
    gain_kn = jnp.tile(jnp.concatenate([p['g_kn_a'], jnp.zeros((HEAD_BLOCK - A_NOPE,), F32)]), A_HEADS)
    gain_kr = jnp.concatenate([jnp.zeros((A_NOPE,), F32), p['g_kr_a'], z32])
    gain_qb = jnp.concatenate(gqb) * (B_SCALE * LOG2E)
    gain_kb = jnp.tile(p['g_k_b'], B_KV_HEADS)

    qa_groups = []
    for h in range(A_HEADS):
        qa_groups += [(h * HEAD_BLOCK, A_NOPE), (h * HEAD_BLOCK + A_NOPE, A_ROPE)]
    kn_groups = [(h * HEAD_BLOCK, A_NOPE) for h in range(A_HEADS)]
    qb_groups = [(h * HEAD_BLOCK + B_HEAD_DIM * _b_half(h), B_HEAD_DIM) for h in range(B_HEADS)]
    kb_groups = [(g * B_HEAD_DIM, B_HEAD_DIM) for g in range(B_KV_HEADS)]

    row = lambda v: v.reshape(1, -1).astype(F32)
    b16 = lambda m: m.astype(BF16)
    proj = [row(p['g_attn']),
            b16(w_qa), row(p['g_q_a']), b16(w_uq), *_group_mats(A_HEADS * HEAD_BLOCK, qa_groups), row(gain_qa),
            b16(w_ckv), row(p['g_kv_a']), b16(w_kr), row(gain_kr),
            b16(w_uk), *_group_mats(A_HEADS * HEAD_BLOCK, kn_groups), row(gain_kn), b16(w_uv),
            b16(w_qb), *_group_mats(B_HEADS * HEAD_BLOCK, qb_groups), row(gain_qb),
            b16(w_kb), *_group_mats(B_KV_HEADS * B_HEAD_DIM, kb_groups), row(gain_kb),
            b16(w_vb), b16(w_qi), b16(w_ki), b16(w_wi)]
    extra = dict(w_gate=b16(w_gate), w_uk_blocks=b16(w_uk), w_uk_compact=b16(w_uk_c), w_uv=b16(w_uv),
                 g_kn=row(p['g_kn_a']))
    return proj, extra


def _proj_call(x, tables, proj_params):
    m = x.shape[0]
    t = min(PROJ_TILE, m)
    assert m % t == 0 and tables[0].shape[0] % t == 0
    nt = tables[0].shape[0] // t
    row_spec = lambda w: pl.BlockSpec((t, w), lambda i: (i, 0))
    tab_spec = pl.BlockSpec((t, LANES), lambda i: (i % nt, 0))
    hb = A_HEADS * HEAD_BLOCK
    outs = [(hb, BF16), (hb, BF16), (A_HEADS * A_V, BF16), (A_KV_RANK, F32), (A_ROPE, F32),
            (B_HEADS * HEAD_BLOCK, BF16), (B_KV_HEADS * B_HEAD_DIM, F32), (B_KV_HEADS * B_HEAD_DIM, BF16),
            (B_KV_HEADS * B_HEAD_DIM, F32), (B_KV_HEADS * B_HEAD_DIM, BF16),
            (IDX_HEADS * HEAD_BLOCK, BF16), (IDX_DIM, F32), (LANES, BF16), (LANES, F32)]
    return pl.pallas_call(
        _proj_kernel,
        grid=(m // t,),
        in_specs=[row_spec(D_MODEL), _const_spec(proj_params[0])] + [tab_spec] * 6
                 + [_const_spec(a) for a in proj_params[1:]],
        out_specs=[row_spec(w) for w, _ in outs],
        out_shape=[jax.ShapeDtypeStruct((m, w), dt) for w, dt in outs],
        compiler_params=pltpu.CompilerParams(dimension_semantics=("arbitrary",), vmem_limit_bytes=VMEM_LIMIT),
        name="proj",
    )(x, proj_params[0], *tables, *proj_params[1:])


def _mla_prompt_kernel(q_ref, k_ref, v_ref, o_ref, *, tq):
    i = pl.program_id(2)
    rows = lax.broadcasted_iota(I32, (tq, tq), 0)
    cols = lax.broadcasted_iota(I32, (tq, tq), 1)
    causal = cols <= rows
    lane = lax.broadcasted_iota(I32, (tq, LANES), 1)
    qs = [q_ref[:, hh * HEAD_BLOCK:(hh + 1) * HEAD_BLOCK] for hh in range(2)]

    def step(j, carry, masked):
        off = pl.multiple_of(j * tq, tq)
        v = v_ref[pl.ds(off, tq), :]
        new = []
        for hh in range(2):
            m, l, acc = carry[hh]
            s = _dot_nt(qs[hh], k_ref[pl.ds(off, tq), hh * HEAD_BLOCK:(hh + 1) * HEAD_BLOCK])
            if masked:
                s = jnp.where(causal, s, NEG)
            m_new = jnp.maximum(m, jnp.max(s, axis=-1, keepdims=True))
            p = jnp.exp2(s - m_new)
            corr = jnp.exp2(m - m_new)
            l = l * corr + jnp.sum(p, axis=-1, keepdims=True)
            new.append((m_new, l, acc * corr + _dot(p.astype(BF16), v)))
        return tuple(new)

    init = tuple((jnp.full((tq, 1), NEG, F32), jnp.zeros((tq, 1), F32), jnp.zeros((tq, LANES), F32))
                 for _ in range(2))
    carry = lax.fori_loop(0, i, functools.partial(step, masked=False), init)
    res = step(i, carry, masked=True)
    outs = [res[hh][2] / res[hh][1] for hh in range(2)]
    o_ref[...] = jnp.where(lane < A_V, outs[0], outs[1]).astype(o_ref.dtype)


def _mla_prompt_call(qa, kcat, va, n, s):
    tq = min(MLA_TQ, s)
    nq = s // tq
    pairs = A_HEADS // 2
    return pl.pallas_call(
        functools.partial(_mla_prompt_kernel, tq=tq),
        grid=(n, pairs, nq),
        in_specs=[pl.BlockSpec((tq, 2 * HEAD_BLOCK), lambda b, h, i: (b * nq + i, h)),
                  pl.BlockSpec((s, 2 * HEAD_BLOCK), lambda b, h, i: (b, h)),
                  pl.BlockSpec((s, 2 * A_V), lambda b, h, i: (b, h))],
        out_specs=pl.BlockSpec((tq, 2 * A_V), lambda b, h, i: (b * nq + i, h)),
        out_shape=jax.ShapeDtypeStruct((n * s, A_HEADS * A_V), BF16),
        compiler_params=pltpu.CompilerParams(dimension_semantics=("arbitrary",) * 3, vmem_limit_bytes=VMEM_LIMIT),
        name="mla_prompt",
    )(qa, kcat, va)


def _order_key(score):
    bits = pltpu.bitcast(score, I32)
    return bits ^ ((bits >> 31) & 0x7FFFFFFF)


def _select_threshold(count_ge, count_eq_below, shape, n_all, nsel, nbits, row_ok):
    c0 = count_ge(jnp.zeros(shape, I32))
    thr = jnp.where(c0 >= nsel, 0, INT_MIN).astype(I32)
    n_ge = jnp.where(c0 >= nsel, c0, n_all).astype(I32)

    def bit_body(p, carry):
        thr, n_ge = carry
        cand = thr + lax.shift_left(jnp.int32(1), 30 - p)
        cnt = count_ge(cand)
        ok = cnt >= nsel
        return jnp.where(ok, cand, thr), jnp.where(ok, cnt, n_ge)

    thr, n_ge = lax.fori_loop(0, 31, bit_body, (thr, n_ge))
    tie = (n_ge > nsel) & (thr > INT_MIN) & row_ok
    all_cols = jnp.full(shape, 2 ** 30, I32)

    def resolve():
        need = nsel - jnp.where(thr == 2 ** 31 - 1, 0, count_ge(thr + 1))

        def jbit(p, jmax):
            cand = jmax | lax.shift_left(jnp.int32(1), nbits - 1 - p)
            return jnp.where(count_eq_below(thr, cand) < need, cand, jmax)
        jmax = lax.fori_loop(0, nbits, jbit, jnp.zeros(shape, I32))
        return jnp.where(tie, jmax, all_cols)

    jmax = lax.cond(jnp.max(tie.astype(I32)) > 0, resolve, lambda: all_cols)
    return thr, jmax


def _dsa_prompt_kernel(qi_ref, wi_ref, qb_ref, ki_ref, kb_ref, vb_ref, o_ref, key_scr, *, tq, tk, nsel, nbits):
    i = pl.program_id(1)
    q0 = i * tq
    nch = q0 // tk + 1
    row_g = q0 + lax.broadcasted_iota(I32, (tq, tk), 0)
    col_l = lax.broadcasted_iota(I32, (tq, tk), 1)

    qi = jnp.concatenate([qi_ref[:, h * HEAD_BLOCK:(h + 1) * HEAD_BLOCK] for h in range(IDX_HEADS)], axis=0)
    w = wi_ref[...]
    wcol = [w[:, h:h + 1] for h in range(IDX_HEADS)]

    def idx_chunk(c, carry):
        off = pl.multiple_of(c * tk, tk)
        s = _dot_nt(qi, ki_ref[pl.ds(off, tk), :])
        score = jnp.zeros((tq, tk), F32)
        for h in range(IDX_HEADS):
            score = score + wcol[h] * jnp.maximum(s[h * tq:(h + 1) * tq], 0.0)
        key_scr[:, pl.ds(off, tk)] = jnp.where(off + col_l <= row_g, _order_key(score), INT_MIN)
        return carry

    lax.fori_loop(0, nch, idx_chunk, 0)

    rb = min(COUNT_ROWS, tq)
    lane_l = lax.broadcasted_iota(I32, (rb, LANES), 1)

    def count(pred):
        def body(c, accs):
            off = pl.multiple_of(c * tk, tk)
            new = []
            for b in range(tq // rb):
                sl = slice(b * rb, (b + 1) * rb)
                acc = accs[b]
                for jj in range(tk // LANES):
                    k = key_scr[sl, pl.ds(off + jj * LANES, LANES)]
                    acc = acc + jnp.where(pred(k, off + jj * LANES + lane_l, sl), 1, 0).astype(I32)
                new.append(acc)
            return tuple(new)
        accs = lax.fori_loop(0, nch, body, tuple(jnp.zeros((rb, LANES), I32) for _ in range(tq // rb)))
        tot = jnp.sum(jnp.concatenate(accs, axis=0), axis=-1, keepdims=True)
        return jnp.broadcast_to(tot, (tq, LANES))

    count_ge = lambda cand: count(lambda k, col, sl: k >= cand[sl])
    count_eq_below = lambda thr, j: count(lambda k, col, sl: (k == thr[sl]) & (col < j[sl]))
    thr, jmax = _select_threshold(count_ge, count_eq_below, (tq, LANES), nch * tk, nsel, nbits,
                                  lax.broadcasted_iota(I32, (tq, LANES), 0) >= 0)
    thr, jmax = thr[:, :1], jmax[:, :1]

    hp = B_HEADS // 2
    qs = [jnp.concatenate([qb_ref[:, h * HEAD_BLOCK:(h + 1) * HEAD_BLOCK] for h in range(p * hp, (p + 1) * hp)],
                          axis=0) for p in range(2)]

    def att_chunk(c, carry):
        off = pl.multiple_of(c * tk, tk)
        key = key_scr[:, pl.ds(off, tk)]
        col = off + col_l
        sel = ((key > thr) | ((key == thr) & (col <= jmax))) & (col <= row_g)
        new = []
        for p in range(2):
            m, l, acc = carry[p]
            kc = kb_ref[pl.ds(off, tk), p * LANES:(p + 1) * LANES]
            vc = vb_ref[pl.ds(off, tk), p * LANES:(p + 1) * LANES]
            s = jnp.where(sel[None], _dot_nt(qs[p], kc).reshape(hp, tq, tk), NEG)
            m_new = jnp.maximum(m, jnp.max(s, axis=-1, keepdims=True))
            pe = jnp.exp2(s - m_new)
            corr = jnp.exp2(m - m_new)
            l = l * corr + jnp.sum(pe, axis=-1, keepdims=True)
            pv = _dot(pe.reshape(hp * tq, tk).astype(BF16), vc).reshape(hp, tq, LANES)
            new.append((m_new, l, acc * corr + pv))
        return tuple(new)

    init = tuple((jnp.full((hp, tq, 1), NEG, F32), jnp.zeros((hp, tq, 1), F32), jnp.zeros((hp, tq, LANES), F32))
                 for _ in range(2))
    res = lax.fori_loop(0, nch, att_chunk, init)
    o_ref[...] = _assemble_b_heads([res[p][2] / res[p][1] for p in range(2)], tq).astype(o_ref.dtype)


def _assemble_b_heads(accs, rows):
    lane = lax.broadcasted_iota(I32, (rows, LANES), 1)
    swap = lambda a: pltpu.roll(a, B_HEAD_DIM, 1)
    blocks = []
    for p in range(2):
        a = accs[p]
        blocks.append(jnp.where(lane < B_HEAD_DIM, a[0], swap(a[1])))
        blocks.append(jnp.where(lane < B_HEAD_DIM, swap(a[2]), a[3]))
    return jnp.concatenate(blocks, axis=1)


def _dsa_prompt_call(qi, wi, qb, ki16, kb16, vb16, n, s):
    tq = min(DSA_TQ, s)
    tk = min(DSA_TK, s)
    nq = s // tq
    nsel = min(TOPK_MAX, s // 4)
    kvw = B_KV_HEADS * B_HEAD_DIM
    qspec = lambda w: pl.BlockSpec((tq, w), lambda b, i: (b * nq + i, 0))
    kspec = lambda w: pl.BlockSpec((s, w), lambda b, i: (b, 0))
    return pl.pallas_call(
        functools.partial(_dsa_prompt_kernel, tq=tq, tk=tk, nsel=nsel, nbits=int(s).bit_length()),
        grid=(n, nq),
        in_specs=[qspec(IDX_HEADS * HEAD_BLOCK), qspec(LANES), qspec(B_HEADS * HEAD_BLOCK),
                  kspec(LANES), kspec(kvw), kspec(kvw)],
        out_specs=qspec(B_HEADS * B_HEAD_DIM),
        out_shape=jax.ShapeDtypeStruct((n * s, B_HEADS * B_HEAD_DIM), BF16),
        scratch_shapes=[pltpu.VMEM((tq, s), I32)],
        compiler_params=pltpu.CompilerParams(dimension_semantics=("arbitrary",) * 2, vmem_limit_bytes=VMEM_LIMIT),
        name="dsa_prompt",
    )(qi, wi, qb, ki16, kb16, vb16)


def _mla_sample_kernel(pt_ref, q_ref, latn_ref, krn_ref, qmask_ref, wukb_ref, wukt_ref, wuv_ref,
                       hmask_ref, place_ref, lat_hbm, kr_hbm, o_ref, lhs_scr, qr_scr, m_scr, l_scr, acc_scr,
                       lat_buf, kr_buf, sem, *, pps, ppc, page, n_tok, nseq, nj):
    n = pl.program_id(0)
    j = pl.program_id(1)
    rows = A_HEADS * T_PAD
    knw = A_HEADS * A_NOPE

    def page_copies(seq, jj, slot):
        cps = []
        for k in range(pps):
            pid = pt_ref[seq, jj * pps + k]
            cps.append(pltpu.make_async_copy(lat_hbm.at[pid], lat_buf.at[slot, k], sem.at[0, slot]))
            cps.append(pltpu.make_async_copy(kr_hbm.at[pid], kr_buf.at[slot, k], sem.at[1, slot]))
        return cps

    step = n * nj + j
    slot = step & 1

    @pl.when(step == 0)
    def _():
        _start_pages(page_copies(n, j, slot), 2)

    @pl.when(step + 1 < nseq * nj)
    def _():
        wrap = j + 1 == nj
        _start_pages(page_copies(jnp.where(wrap, n + 1, n), jnp.where(wrap, 0, j + 1), 1 - slot), 2)

    for cp in page_copies(n, j, slot):
        cp.wait()

    @pl.when(j == 0)
    def _():
        q8 = q_ref[0].astype(F32)
        qpad = (jnp.concatenate([q8] * A_HEADS, axis=0) * qmask_ref[...]).astype(BF16)
        lhs_scr[:knw, :] = wukt_ref[...]
        lhs_scr[knw:, :] = _dot_nt(qpad, wukb_ref[...]).astype(BF16)
        qr = jnp.concatenate([q8[:, h * HEAD_BLOCK:(h + 1) * HEAD_BLOCK] for h in range(A_HEADS)], axis=0)
        qr_scr[...] = _dot(qr.astype(BF16), place_ref[...]).astype(BF16)
        m_scr[...] = jnp.full((rows, 1), NEG, F32)
        l_scr[...] = jnp.zeros((rows, 1), F32)
        acc_scr[...] = jnp.zeros((rows, A_KV_RANK), F32)

    def partial(c, krt, mask):
        nk = c.shape[0]
        c16 = c.astype(BF16)
        both = _dot_nt(lhs_scr[...], c16)
        knt = both[:knw]
        msq = jnp.sum((knt * knt).reshape(A_HEADS, A_NOPE, nk), axis=1) * (1.0 / A_NOPE)
        r = lax.rsqrt(msq + EPS)
        r = jnp.broadcast_to(r[:, None, :], (A_HEADS, T_PAD, nk)).reshape(rows, nk)
        s = both[knw:] * r + _dot(qr_scr[...], krt.astype(BF16))
        if mask is not None:
            s = jnp.where(mask, s, NEG)
        mg = jnp.max(s, axis=-1, keepdims=True)
        p = jnp.exp2(s - mg)
        return mg, jnp.sum(p, axis=-1, keepdims=True), _dot(p.astype(BF16), c16)

    def merge(parts):
        m = m_scr[...]
        m_new = m
        for mg, _, _ in parts:
            m_new = jnp.maximum(m_new, mg)
        corr = jnp.exp2(m - m_new)
        l = l_scr[...] * corr
        acc = acc_scr[...] * corr
        for mg, lg, pv in parts:
            w = jnp.exp2(mg - m_new)
            l = l + lg * w
            acc = acc + pv * w
        m_scr[...] = m_new
        l_scr[...] = l
        acc_scr[...] = acc

    parts = []
    for g in range(pps // ppc):
        c = jnp.concatenate([lat_buf[slot, g * ppc + k] for k in range(ppc)], axis=0)
        krt = jnp.concatenate([kr_buf[slot, g * ppc + k] for k in range(ppc)], axis=1)
        parts.append(partial(c, krt, None))
    merge(parts)

    @pl.when(j == pl.num_programs(1) - 1)
    def _():
        r = lax.broadcasted_iota(I32, (rows, page), 0) & (T_PAD - 1)
        col = lax.broadcasted_iota(I32, (rows, page), 1)
        merge([partial(latn_ref[0], krn_ref[0], (col <= r) & (col < n_tok))])
        lat_out = (acc_scr[...] / l_scr[...]).astype(BF16)
        full = _dot(lat_out, wuv_ref[...])
        hm = hmask_ref[...]
        out = jnp.zeros((T_PAD, A_HEADS * A_V), F32)
        for h in range(A_HEADS):
            out = out + full[h * T_PAD:(h + 1) * T_PAD] * hm[h:h + 1]
        o_ref[0] = out.astype(o_ref.dtype)


def _start_pages(copies, per_page):
    for idx, cp in enumerate(copies):
        cp.start(priority=(idx // per_page) % DMA_PRIORITIES)


def _pad_tokens(a, nseq, t, rows=T_PAD):
    a = a.reshape(nseq, t, a.shape[-1])
    return jnp.pad(a, ((0, 0), (0, rows - t), (0, 0)))


def _new_page_t(a, nseq, t, page):
    return jnp.swapaxes(_pad_tokens(a, nseq, t, page), 1, 2)


def _pages_t(cache):
    pool, page = cache.shape[:2]
    return jnp.moveaxis(cache, 1, -1).reshape(pool, -1, page)


def _mla_sample_call(page_table, qa, lat_new, kr_new, cache_lat, cache_kr_t, extra, t):
    nseq, npages = page_table.shape
    page = cache_lat.shape[1]
    pps = min(PAGES_PER_STEP, npages)
    ppc = min(PAGES_PER_CHUNK, pps)
    assert npages % pps == 0 and pps % ppc == 0
    rows = A_HEADS * T_PAD
    hb = A_HEADS * HEAD_BLOCK

    qmask = np.zeros((rows, hb), np.float32)
    hmask = np.zeros((A_HEADS, A_HEADS * A_V), np.float32)
    for h in range(A_HEADS):
        qmask[h * T_PAD:(h + 1) * T_PAD, h * HEAD_BLOCK:h * HEAD_BLOCK + A_NOPE] = 1.0
        hmask[h, h * A_V:(h + 1) * A_V] = 1.0
    gkn_blocks = jnp.tile(jnp.pad(extra['g_kn'], ((0, 0), (0, HEAD_BLOCK - A_NOPE))), (1, A_HEADS))
    qmask = jnp.asarray(qmask) * gkn_blocks
    place = np.zeros((HEAD_BLOCK, A_ROPE), np.float32)
    place[A_NOPE:A_NOPE + A_ROPE, :] = np.eye(A_ROPE, dtype=np.float32)
    consts = [qmask, extra['w_uk_blocks'], extra['w_uk_compact'].T, extra['w_uv'],
              jnp.asarray(hmask), jnp.asarray(place, BF16)]

    seq_spec = lambda r, w: pl.BlockSpec((1, r, w), lambda n, j, pt: (n, 0, 0))
    cspec = lambda a: pl.BlockSpec(a.shape, lambda n, j, pt: (0,) * a.ndim)
    hbm_spec = pl.BlockSpec(memory_space=pl.ANY)
    nj = npages // pps
    grid_spec = pltpu.PrefetchScalarGridSpec(
        num_scalar_prefetch=1,
        grid=(nseq, nj),
        in_specs=[seq_spec(T_PAD, hb), seq_spec(page, A_KV_RANK), seq_spec(A_ROPE, page)] + [cspec(a) for a in consts]
                 + [hbm_spec, hbm_spec],
        out_specs=seq_spec(T_PAD, A_HEADS * A_V),
        scratch_shapes=[pltpu.VMEM((A_HEADS * A_NOPE + rows, A_KV_RANK), BF16), pltpu.VMEM((rows, A_ROPE), BF16),
                        pltpu.VMEM((rows, 1), F32), pltpu.VMEM((rows, 1), F32), pltpu.VMEM((rows, A_KV_RANK), F32),
                        pltpu.VMEM((2, pps, page, A_KV_RANK), F32), pltpu.VMEM((2, pps, A_ROPE, page), F32),
                        pltpu.SemaphoreType.DMA((2, 2))],
    )
    out = pl.pallas_call(
        functools.partial(_mla_sample_kernel, pps=pps, ppc=ppc, page=page, n_tok=t, nseq=nseq, nj=nj),
        grid_spec=grid_spec,
        out_shape=jax.ShapeDtypeStruct((nseq, T_PAD, A_HEADS * A_V), BF16),
        compiler_params=pltpu.CompilerParams(dimension_semantics=("arbitrary",) * 2, vmem_limit_bytes=VMEM_LIMIT),
        name="mla_sample",
    )(page_table, _pad_tokens(qa, nseq, t), _pad_tokens(lat_new, nseq, t, page), _new_page_t(kr_new, nseq, t, page),
      *consts, cache_lat, cache_kr_t)
    return out[:, :t].reshape(nseq * t, A_HEADS * A_V)


def _dsa_sample_kernel(pt_ref, qi_ref, wi_ref, qb_ref, kin_ref, kbn_ref, vbn_ref, ik_hbm, k_hbm, v_hbm,
                       o_ref, key_scr, thr_scr, jmax_scr, m_scr, l_scr, acc_scr, ik_buf, k_buf, v_buf, sem,
                       *, pps, page, nsel, nbits, n_tok, past, ngroups, gsz, nj):
    g = pl.program_id(0)
    phase = pl.program_id(1)
    si = pl.program_id(2)
    j = pl.program_id(3)
    n = g * gsz + si
    last = nj - 1
    kt = pps * page
    hp = B_HEADS // 2
    tok = lax.broadcasted_iota(I32, (T_PAD, 1), 0)
    row0 = pl.multiple_of(si * T_PAD, T_PAD)

    def page_copies(seq, ph, jj, slot):
        cps = []
        for k in range(pps):
            pid = pt_ref[seq, jj * pps + k]
            if ph == 0:
                cps.append(pltpu.make_async_copy(ik_hbm.at[pid], ik_buf.at[slot, k], sem.at[0, slot]))
            else:
                cps.append(pltpu.make_async_copy(k_hbm.at[pid], k_buf.at[slot, k], sem.at[1, slot]))
                cps.append(pltpu.make_async_copy(v_hbm.at[pid], v_buf.at[slot, k], sem.at[2, slot]))
        return cps

    def for_phase(ph_value, fn):
        for ph in range(2):
            pl.when(ph_value == ph)(functools.partial(fn, ph))

    step = ((g * 2 + phase) * gsz + si) * nj + j
    slot = step & 1

    def start_all(ph, seq, jj, slot):
        _start_pages(page_copies(seq, ph, jj, slot), 1 + ph)

    @pl.when(step == 0)
    def _():
        start_all(0, n, j, slot)

    @pl.when(step + 1 < ngroups * 2 * gsz * nj)
    def _():
        wrap_j = j + 1 == nj
        wrap_s = wrap_j & (si + 1 == gsz)
        next_si = jnp.where(wrap_j, jnp.where(wrap_s, 0, si + 1), si)
        next_ph = jnp.where(wrap_s, 1 - phase, phase)
        next_g = jnp.where(wrap_s & (phase == 1), g + 1, g)
        for_phase(next_ph, functools.partial(start_all, seq=next_g * gsz + next_si,
                                             jj=jnp.where(wrap_j, 0, j + 1), slot=1 - slot))

    def wait_all(ph):
        for cp in page_copies(n, ph, j, slot):
            cp.wait()

    for_phase(phase, wait_all)

    def new_visible(width):
        col = lax.broadcasted_iota(I32, (T_PAD, width), 1)
        return (col <= tok) & (col < n_tok)

    @pl.when(phase == 0)
    def _():
        q8 = qi_ref[0].astype(F32)
        qi = jnp.concatenate([q8[:, h * HEAD_BLOCK:h * HEAD_BLOCK + IDX_DIM] for h in range(IDX_HEADS)],
                             axis=0).astype(BF16)
        w = wi_ref[0]

        def score(ikt16):
            s = _dot(qi, ikt16)
            tot = jnp.zeros((T_PAD, s.shape[1]), F32)
            for h in range(IDX_HEADS):
                tot = tot + w[:, h:h + 1] * jnp.maximum(s[h * T_PAD:(h + 1) * T_PAD], 0.0)
            return _order_key(tot)

        ikt = jnp.concatenate([ik_buf[slot, k] for k in range(pps)], axis=1).astype(BF16)
        key_scr[pl.ds(row0, T_PAD), pl.ds(pl.multiple_of(j * kt, kt), kt)] = score(ikt)

        @pl.when(j == last)
        def _():
            knew = score(kin_ref[0])
            key_scr[pl.ds(row0, T_PAD), past:past + page] = jnp.where(new_visible(page), knew, INT_MIN)

        @pl.when((j == last) & (si == gsz - 1))
        def _():
            grows = gsz * T_PAD
            col = lax.broadcasted_iota(I32, (grows, past + page), 1)
            row_tok = lax.broadcasted_iota(I32, (grows, 1), 0) & (T_PAD - 1)

            def count_ge(cand):
                return jnp.sum(jnp.where(key_scr[...] >= cand, 1, 0).astype(I32), axis=-1, keepdims=True)

            def count_eq_below(thr, jj):
                hit = (key_scr[...] == thr) & (col < jj)
                return jnp.sum(jnp.where(hit, 1, 0).astype(I32), axis=-1, keepdims=True)

            thr, jmax = _select_threshold(count_ge, count_eq_below, (grows, 1), past + page, nsel, nbits,
                                          row_tok < n_tok)
            thr_scr[...] = thr
            jmax_scr[...] = jmax

    @pl.when(phase == 1)
    def _():
        @pl.when(j == 0)
        def _():
            m_scr[...] = jnp.full(m_scr.shape, NEG, F32)
            l_scr[...] = jnp.zeros(l_scr.shape, F32)
            acc_scr[...] = jnp.zeros(acc_scr.shape, F32)

        q8 = qb_ref[0].astype(F32)
        qs = [jnp.concatenate([q8[:, h * HEAD_BLOCK:(h + 1) * HEAD_BLOCK] for h in range(p * hp, (p + 1) * hp)],
                              axis=0).astype(BF16) for p in range(2)]
        thr = thr_scr[pl.ds(row0, T_PAD), :]
        jmax = jmax_scr[pl.ds(row0, T_PAD), :]

        def attend(kt16, vt16, key, col0, extra_mask):
            width = key.shape[1]
            col = col0 + lax.broadcasted_iota(I32, (T_PAD, width), 1)
            sel = (key > thr) | ((key == thr) & (col <= jmax))
            if extra_mask is not None:
                sel = sel & extra_mask
            for p in range(2):
                s = _dot(qs[p], kt16[p * LANES:(p + 1) * LANES]).reshape(hp, T_PAD, width)
                s = jnp.where(sel[None], s, NEG)
                m = m_scr[p]
                m_new = jnp.maximum(m, jnp.max(s, axis=-1, keepdims=True))
                pe = jnp.exp2(s - m_new)
                corr = jnp.exp2(m - m_new)
                l_scr[p] = l_scr[p] * corr + jnp.sum(pe, axis=-1, keepdims=True)
                pv = _dot_nt(pe.reshape(hp * T_PAD, width).astype(BF16), vt16[p * LANES:(p + 1) * LANES])
                acc_scr[p] = acc_scr[p] * corr + pv.reshape(hp, T_PAD, LANES)
                m_scr[p] = m_new

        kt16 = jnp.concatenate([k_buf[slot, k] for k in range(pps)], axis=1).astype(BF16)
        vt16 = jnp.concatenate([v_buf[slot, k] for k in range(pps)], axis=1).astype(BF16)
        off = pl.multiple_of(j * kt, kt)
        attend(kt16, vt16, key_scr[pl.ds(row0, T_PAD), pl.ds(off, kt)], off, None)

        @pl.when(j == last)
        def _():
            attend(kbn_ref[0], vbn_ref[0], key_scr[pl.ds(row0, T_PAD), past:past + page], past, new_visible(page))
            o_ref[0] = _assemble_b_heads([acc_scr[p] / l_scr[p] for p in range(2)], T_PAD).astype(o_ref.dtype)


def _dsa_sample_call(page_table, qi, wi, qb, ki16, kb16, vb16, cache_ik_t, cache_k_t, cache_v_t, t):
    nseq, npages = page_table.shape
    page = cache_ik_t.shape[2]
    past = npages * page
    pps = min(PAGES_PER_STEP, npages)
    assert npages % pps == 0 and t <= T_PAD
    nj = npages // pps
    nsel = min(TOPK_MAX, (past + t) // 4)
    kvw = B_KV_HEADS * B_HEAD_DIM
    hp = B_HEADS // 2

    new_page = lambda a: _new_page_t(a, nseq, t, page)
    gsz = math.gcd(SELECT_GROUP, nseq)
    ngroups = nseq // gsz

    seq_spec = lambda r, w: pl.BlockSpec((1, r, w), lambda g, ph, si, j, pt: (g * gsz + si, 0, 0))
    out_spec = pl.BlockSpec((1, T_PAD, B_HEADS * B_HEAD_DIM), lambda g, ph, si, j, pt: (g * gsz + si * ph, 0, 0))
    hbm_spec = pl.BlockSpec(memory_space=pl.ANY)
    grid_spec = pltpu.PrefetchScalarGridSpec(
        num_scalar_prefetch=1,
        grid=(ngroups, 2, gsz, nj),
        in_specs=[seq_spec(T_PAD, IDX_HEADS * HEAD_BLOCK), seq_spec(T_PAD, LANES), seq_spec(T_PAD, B_HEADS * HEAD_BLOCK),
                  seq_spec(IDX_DIM, page), seq_spec(kvw, page), seq_spec(kvw, page)] + [hbm_spec] * 3,
        out_specs=out_spec,
        scratch_shapes=[pltpu.VMEM((gsz * T_PAD, past + page), I32), pltpu.VMEM((gsz * T_PAD, 1), I32),
                        pltpu.VMEM((gsz * T_PAD, 1), I32),
                        pltpu.VMEM((2, hp, T_PAD, 1), F32), pltpu.VMEM((2, hp, T_PAD, 1), F32),
                        pltpu.VMEM((2, hp, T_PAD, LANES), F32),
                        pltpu.VMEM((2, pps, IDX_DIM, page), F32), pltpu.VMEM((2, pps, kvw, page), F32),
                        pltpu.VMEM((2, pps, kvw, page), F32), pltpu.SemaphoreType.DMA((3, 2))],
    )
    out = pl.pallas_call(
        functools.partial(_dsa_sample_kernel, pps=pps, page=page, nsel=nsel, nbits=int(past + page).bit_length(),
                          n_tok=t, past=past, ngroups=ngroups, gsz=gsz, nj=nj),
        grid_spec=grid_spec,
        out_shape=jax.ShapeDtypeStruct((nseq, T_PAD, B_HEADS * B_HEAD_DIM), BF16),
        compiler_params=pltpu.CompilerParams(dimension_semantics=("arbitrary",) * 4, vmem_limit_bytes=VMEM_LIMIT),
        name="dsa_sample",
    )(page_table, _pad_tokens(qi, nseq, t), _pad_tokens(wi, nseq, t), _pad_tokens(qb, nseq, t),
      new_page(ki16[:, :IDX_DIM]), new_page(kb16), new_page(vb16),
      cache_ik_t, cache_k_t, cache_v_t)
    return out[:, :t].reshape(nseq * t, B_HEADS * B_HEAD_DIM)


def _merge_kernel(x_ref, oa_ref, ob_ref, gattn_ref, wgate_ref, woa_ref, wob_ref, wout_ref, h_ref):
    x = x_ref[...]
    n = _rms_full(x, gattn_ref[...]).astype(BF16)
    gate = jax.nn.sigmoid(_dot(n, wgate_ref[...]))
    mix = gate[:, :D_MODEL] * _dot(oa_ref[...], woa_ref[...]) + gate[:, D_MODEL:] * _dot(ob_ref[...], wob_ref[...])
    h_ref[...] = x + _dot(mix.astype(BF16), wout_ref[...])


def _mlp_kernel(h_ref, gmlp_ref, wup_ref, wdown_ref, y_ref):
    h = h_ref[...]
    u = jnp.maximum(_dot(_rms_full(h, gmlp_ref[...]).astype(BF16), wup_ref[...]), 0.0)
    y_ref[...] = h + _dot((u * u).astype(BF16), wdown_ref[...])


def _output_call(x, o_a, o_b, g_attn, w_gate, p):
    m = x.shape[0]
    t = min(PROJ_TILE, m)
    row_spec = lambda w: pl.BlockSpec((t, w), lambda i: (i, 0))
    params = pltpu.CompilerParams(dimension_semantics=("arbitrary",), vmem_limit_bytes=VMEM_LIMIT)
    consts = [g_attn, w_gate, p['w_o_a'].astype(BF16), p['w_o_b'].astype(BF16), p['w_out'].astype(BF16)]
    h = pl.pallas_call(
        _merge_kernel,
        grid=(m // t,),
        in_specs=[row_spec(D_MODEL), row_spec(o_a.shape[1]), row_spec(o_b.shape[1])] + [_const_spec(a) for a in consts],
        out_specs=row_spec(D_MODEL),
        out_shape=jax.ShapeDtypeStruct((m, D_MODEL), F32),
        compiler_params=params,
        name="merge",
    )(x, o_a, o_b, *consts)
    consts = [p['g_mlp'].reshape(1, -1), p['w_up'].astype(BF16), p['w_down'].astype(BF16)]
    return pl.pallas_call(
        _mlp_kernel,
        grid=(m // t,),
        in_specs=[row_spec(D_MODEL)] + [_const_spec(a) for a in consts],
        out_specs=row_spec(D_MODEL),
        out_shape=jax.ShapeDtypeStruct((m, D_MODEL), F32),
        compiler_params=params,
        name="mlp",
    )(h, *consts)


def _layer(h_p, h_s, caches, page_table, p):
    n, s, _ = h_p.shape
    nseq, t, _ = h_s.shape
    past = page_table.shape[1] * caches['lat'].shape[1]
    proj_params, extra = _prep_proj_params(p)
    g_attn = proj_params[0]

    xs = h_s.reshape(nseq * t, D_MODEL)
    pos_s = jnp.tile(past + jnp.arange(t), nseq)
    (qa, kcat, va, lat, krope, qb, kb32, kb16, vb32, vb16, qi, ki32, ki16, wi) = _proj_call(
        xs, _rope_tables(pos_s), proj_params)
    o_a = _mla_sample_call(page_table, qa, lat, krope, caches['lat'], _pages_t(caches['kr']), extra, t)
    o_b = _dsa_sample_call(page_table, qi, wi, qb, ki16, kb16, vb16, _pages_t(caches['ik']),
                           _pages_t(caches['k']), _pages_t(caches['v']), t)
    y_s = _output_call(xs, o_a, o_b, g_attn, extra['w_gate'], p).reshape(nseq, t, D_MODEL)
    new_s = (lat.reshape(nseq, t, -1), krope.reshape(nseq, t, -1), kb32.reshape(nseq, t, B_KV_HEADS, B_HEAD_DIM),
             vb32.reshape(nseq, t, B_KV_HEADS, B_HEAD_DIM), ki32.reshape(nseq, t, -1))

    xp = h_p.reshape(n * s, D_MODEL)
    (qa, kcat, va, lat, krope, qb, kb32, kb16, vb32, vb16, qi, ki32, ki16, wi) = _proj_call(
        xp, _rope_tables(jnp.arange(s)), proj_params)
    o_a = _mla_prompt_call(qa, kcat, va, n, s)
    o_b = _dsa_prompt_call(qi, wi, qb, ki16, kb16, vb16, n, s)
    y_p = _output_call(xp, o_a, o_b, g_attn, extra['w_gate'], p).reshape(n, s, D_MODEL)
    new_p = (lat.reshape(n, s, -1), krope.reshape(n, s, -1), kb32.reshape(n, s, B_KV_HEADS, B_HEAD_DIM),
             vb32.reshape(n, s, B_KV_HEADS, B_HEAD_DIM), ki32.reshape(n, s, -1))
    return y_p, y_s, new_p, new_s


def kernel(x_prompt, x_sample, cache_mla_latent, cache_mla_krope, cache_dsa_k, cache_dsa_v, cache_idx_k, page_table, g_attn, w_in, g_q_a, w_uq, g_qn_a, g_qr_a, g_kv_a, g_kr_a, w_uk, g_kn_a, w_uv, w_o_a, g_q_b, g_k_b, w_o_b, w_out, g_mlp, w_up, w_down):
    depth = w_in.shape[0]
    h_p, h_s = x_prompt, x_sample
    new_p, new_s = [], []
    for l in range(depth):
        p = dict(g_attn=g_attn[l], w_in=w_in[l], g_q_a=g_q_a[l], w_uq=w_uq[l], g_qn_a=g_qn_a[l],
                 g_qr_a=g_qr_a[l], g_kv_a=g_kv_a[l], g_kr_a=g_kr_a[l], w_uk=w_uk[l], g_kn_a=g_kn_a[l],
                 w_uv=w_uv[l], w_o_a=w_o_a[l], g_q_b=g_q_b[l], g_k_b=g_k_b[l], w_o_b=w_o_b[l],
                 w_out=w_out[l], g_mlp=g_mlp[l], w_up=w_up[l], w_down=w_down[l])
        caches = dict(lat=cache_mla_latent[l], kr=cache_mla_krope[l], k=cache_dsa_k[l], v=cache_dsa_v[l],
                      ik=cache_idx_k[l])
        h_p, h_s, np_l, ns_l = _layer(h_p, h_s, caches, page_table, p)
        new_p.append(np_l)
        new_s.append(ns_l)
    stack = lambda parts, i: jnp.stack([q[i] for q in parts])
    return (h_p, h_s, *[stack(new_p, i) for i in range(5)], *[stack(new_s, i) for i in range(5)])
```
